```python
import jax
import jax.numpy as jnp
from jax import lax
import numpy as np

D_MODEL = 4096
BATCH = 1
SEQ = 16384
DEPTH = 2
DEC_BATCH = 16
DEC_SEQ = 32
PAST_LEN = 2048

CHUNK = 64
HEAD_DIM = 128
D_MIX = D_MODEL
W_CONV = D_MIX // 4
N_RET_HEADS = D_MIX // 4 // HEAD_DIM
N_FOX_HEADS = D_MIX // 4 // HEAD_DIM
N_BAND_HEADS = D_MIX // 4 // HEAD_DIM
D_RET = N_RET_HEADS * HEAD_DIM
D_FOX = N_FOX_HEADS * HEAD_DIM
D_BAND = N_BAND_HEADS * HEAD_DIM
CONV_WIDTH = 31
BAND_CHUNKS = 8
BAND_PAST = BAND_CHUNKS * CHUNK
REL_CLIP = 128
FOX_QBLOCK = 128
ROPE_BASE = 10000.0
LN_EPS = 1e-5
IN_WIDTHS = (W_CONV, W_CONV, W_CONV, D_RET, D_RET, D_RET, D_RET, D_FOX, D_FOX, D_FOX, D_FOX, N_FOX_HEADS, D_BAND, D_BAND, D_BAND, D_BAND)
D_IN = sum(IN_WIDTHS)

kernel_name = 'hybrid_streaming_encoder_step'


def layer_norm(x, g, b):
    xf = x.astype(jnp.float32)
    mu = jnp.mean(xf, axis=-1, keepdims=True)
    var = jnp.mean(jnp.square(xf - mu), axis=-1, keepdims=True)
    return (xf - mu) * lax.rsqrt(var + LN_EPS) * g.astype(jnp.float32) + b.astype(jnp.float32)


def heads(t):
    return t.reshape(t.shape[0], t.shape[1], -1, HEAD_DIM)


def rotary(x, pos):
    half = HEAD_DIM // 2
    inv = ROPE_BASE ** (-jnp.arange(half, dtype=jnp.float32) / half)
    ang = pos.astype(jnp.float32)[:, None] * inv[None, :]
    cos = jnp.cos(ang)[None, :, None, :]
    sin = jnp.sin(ang)[None, :, None, :]
    xf = x.astype(jnp.float32)
    x1, x2 = xf[..., :half], xf[..., half:]
    return jnp.concatenate([x1 * cos - x2 * sin, x1 * sin + x2 * cos], axis=-1)


def causal_dwconv(u, buf, w, b):
    full = jnp.concatenate([buf.astype(u.dtype), u], axis=1)
    y = lax.conv_general_dilated(full, w.astype(u.dtype)[:, None, :], window_strides=(1,), padding='VALID',
                                 dimension_numbers=('NWC', 'WIO', 'NWC'), feature_group_count=u.shape[-1])
    return y + b.astype(u.dtype), full[:, -(CONV_WIDTH - 1):]


def conv_branch(a_val, a_glu, buf, w, b, g, beta):
    u = a_val * jax.nn.sigmoid(a_glu)
    y, new_buf = causal_dwconv(u, buf, w, b)
    return jax.nn.silu(layer_norm(y, g, beta)), new_buf


def retention(q, k, v, s0, block):
    bsz, t_len, n_h, d = q.shape
    n_blk = t_len // block
    log_gamma = jnp.log1p(-jnp.exp2(-5.0 - jnp.arange(n_h, dtype=jnp.float32)))
    idx = jnp.arange(block)
    diff = idx[:, None] - idx[None, :]
    decay = jnp.where(diff[None] >= 0, jnp.exp(jnp.maximum(diff, 0)[None] * log_gamma[:, None, None]), 0.0)
    q_dec = jnp.exp((idx + 1)[:, None] * log_gamma[None, :])[None, :, :, None]
    k_dec = jnp.exp((block - 1 - idx)[:, None] * log_gamma[None, :])[None, :, :, None]
    s_dec = jnp.exp(block * log_gamma)[None, :, None, None]

    def to_blocks(a):
        return a.astype(jnp.float32).reshape(bsz, n_blk, block, n_h, d).swapaxes(0, 1)

    def step(s, blk):
        qb, kb, vb = blk
        inner = jnp.einsum('blhd,bmhd->bhlm', qb, kb) * decay
        o = jnp.einsum('bhlm,bmhe->blhe', inner, vb) + jnp.einsum('blhd,bhde->blhe', qb * q_dec, s)
        s = s_dec * s + jnp.einsum('blhd,blhe->bhde', kb * k_dec, vb)
        return s, o

    s, o = lax.scan(step, s0.astype(jnp.float32), (to_blocks(q), to_blocks(k), to_blocks(v)))
    return o.swapaxes(0, 1).reshape(bsz, t_len, n_h, d), s


def head_group_norm(o, g, b):
    mu = jnp.mean(o, axis=-1, keepdims=True)
    var = jnp.mean(jnp.square(o - mu), axis=-1, keepdims=True)
    on = (o - mu) * lax.rsqrt(var + LN_EPS)
    return on.reshape(o.shape[0], o.shape[1], -1) * g.astype(jnp.float32) + b.astype(jnp.float32)


def fox_attend(q, k, v, cq, ck, q_pos, k_pos):
    s = jnp.einsum('bqhd,bkhd->bhqk', q, k).astype(jnp.float32) * (HEAD_DIM ** -0.5)
    s = s + jnp.swapaxes(cq, 1, 2)[:, :, :, None] - jnp.swapaxes(ck, 1, 2)[:, :, None, :]
    s = jnp.where((k_pos[None, :] <= q_pos[:, None])[None, None], s, -jnp.inf)
    p = jax.nn.softmax(s, axis=-1)
    return jnp.einsum('bhqk,bkhd->bqhd', p, v.astype(jnp.float32))


def fox_prompt(q, k, v, logf):
    bsz, t_len, n_h, d = q.shape
    c = jnp.cumsum(logf, axis=1)
    k_pos = jnp.arange(t_len)

    def one_block(bi):
        s0 = bi * FOX_QBLOCK
        qb = lax.dynamic_slice_in_dim(q, s0, FOX_QBLOCK, axis=1)
        cqb = lax.dynamic_slice_in_dim(c, s0, FOX_QBLOCK, axis=1)
        return fox_attend(qb, k, v, cqb, c, s0 + jnp.arange(FOX_QBLOCK), k_pos)

    o = lax.map(one_block, jnp.arange(t_len // FOX_QBLOCK))
    return o.swapaxes(0, 1).reshape(bsz, t_len, n_h, d)


def fox_sample(q, k, v, logf, k_c, v_c, lf_c):
    past = k_c.shape[1]
    t_len = q.shape[1]
    k_all = jnp.concatenate([k_c.astype(k.dtype), k], axis=1)
    v_all = jnp.concatenate([v_c.astype(v.dtype), v], axis=1)
    c = jnp.cumsum(jnp.concatenate([lf_c.astype(jnp.float32), logf], axis=1), axis=1)
    return fox_attend(q, k_all, v_all, c[:, past:], c, past + jnp.arange(t_len), jnp.arange(past + t_len))


def band_mask(q_pos, k_pos):
    qc = q_pos[:, None] // CHUNK
    kc = k_pos[None, :] // CHUNK
    return (k_pos[None, :] >= 0) & (kc <= qc) & (kc >= qc - BAND_CHUNKS)


def band_attend(q, k, v, q_pos, k_pos, table):
    rel = jnp.clip(q_pos[:, None] - k_pos[None, :], -REL_CLIP, REL_CLIP) + REL_CLIP
    bias = table[:, rel].astype(jnp.float32)
    s = jnp.einsum('bqhd,bkhd->bhqk', q, k).astype(jnp.float32) * (HEAD_DIM ** -0.5) + bias[None]
    s = jnp.where(band_mask(q_pos, k_pos)[None, None], s, -jnp.inf)
    p = jax.nn.softmax(s, axis=-1)
    return jnp.einsum('bhqk,bkhd->bqhd', p, v.astype(jnp.float32))


def band_prompt(q, k, v, table):
    bsz, t_len, n_h, d = q.shape
    span = BAND_PAST + CHUNK
    pad = jnp.zeros((bsz, BAND_PAST, n_h, d), k.dtype)
    kp = jnp.concatenate([pad, k], axis=1)
    vp = jnp.concatenate([pad, v], axis=1)

    def one_chunk(ci):
        s0 = ci * CHUNK
        qb = lax.dynamic_slice_in_dim(q, s0, CHUNK, axis=1)
        kb = lax.dynamic_slice_in_dim(kp, s0, span, axis=1)
        vb = lax.dynamic_slice_in_dim(vp, s0, span, axis=1)
        return band_attend(qb, kb, vb, s0 + jnp.arange(CHUNK), s0 - BAND_PAST + jnp.arange(span), table)

    o = lax.map(one_chunk, jnp.arange(t_len // CHUNK))
    return o.swapaxes(0, 1).reshape(bsz, t_len, n_h, d)


def band_sample(q, k, v, k_c, v_c, table, q_pos):
    rows = k_c.shape[1]
    k_all = jnp.concatenate([k_c.astype(k.dtype), k], axis=1)
    v_all = jnp.concatenate([v_c.astype(v.dtype), v], axis=1)
    k_pos = jnp.concatenate([q_pos[0] - rows + jnp.arange(rows), q_pos])
    return band_attend(q, k_all, v_all, q_pos, k_pos, table)


def trunk_layer(x, hist, pos0, params, alpha):
    (w_in, conv_w, conv_b, conv_ln_g, conv_ln_b, ret_gn_g, ret_gn_b, fox_bf, rel_bias, w_out, ln_g, ln_b) = params
    bsz, t_len, _ = x.shape
    z = jnp.einsum('btd,de->bte', x, w_in)
    (a_val, a_glu, a_gate, r_q, r_k, r_v, r_g, f_q, f_k, f_v, f_g, f_f, b_q, b_k, b_v, b_g) = jnp.split(
        z, np.cumsum(IN_WIDTHS)[:-1].tolist(), axis=-1)
    pos = pos0 + jnp.arange(t_len)
    prompt = hist is None
    if prompt:
        conv_buf = jnp.zeros((bsz, CONV_WIDTH - 1, W_CONV), x.dtype)
        ret_s0 = jnp.zeros((bsz, N_RET_HEADS, HEAD_DIM, HEAD_DIM), jnp.float32)
    else:
        conv_buf, ret_s0, fox_k_c, fox_v_c, fox_lf_c, band_k_c, band_v_c = hist

    y_a, conv_state = conv_branch(a_val, a_glu, conv_buf, conv_w, conv_b, conv_ln_g, conv_ln_b)

    rq = rotary(heads(r_q), pos) * (HEAD_DIM ** -0.5)
    rk = rotary(heads(r_k), pos)
    o_b, ret_state = retention(rq, rk, heads(r_v), ret_s0, CHUNK if prompt else t_len)
    y_b = head_group_norm(o_b, ret_gn_g, ret_gn_b)

    fq, fk, fv = heads(f_q), heads(f_k), heads(f_v)
    logf = jax.nn.log_sigmoid(f_f.astype(jnp.float32) + fox_bf.astype(jnp.float32))
    if prompt:
        o_c = fox_prompt(fq, fk, fv, logf)
    else:
        o_c = fox_sample(fq, fk, fv, logf, fox_k_c, fox_v_c, fox_lf_c)

    bq, bk, bv = heads(b_q), heads(b_k), heads(b_v)
    if prompt:
        o_d = band_prompt(bq, bk, bv, rel_bias)
        keep = min(BAND_PAST, t_len)
        band_k_new, band_v_new = bk[:, t_len - keep:], bv[:, t_len - keep:]
    else:
        o_d = band_sample(bq, bk, bv, band_k_c, band_v_c, rel_bias, pos)
        band_k_new, band_v_new = bk, bv

    mixed = jnp.concatenate([
        y_a * jax.nn.silu(a_gate),
        y_b * jax.nn.silu(r_g),
        o_c.reshape(bsz, t_len, D_FOX) * jax.nn.silu(f_g),
        o_d.reshape(bsz, t_len, D_BAND) * jax.nn.silu(b_g)], axis=-1).astype(x.dtype)
    out = jnp.einsum('bte,ed->btd', mixed, w_out)
    y = layer_norm(alpha * x + out, ln_g, ln_b).astype(x.dtype)
    return y, (conv_state, ret_state, fk, fv, logf, band_k_new, band_v_new)


def _stack(states, i):
    return jnp.stack([s[i] for s in states], axis=0)


def setup_inputs(seed: int = 0) -> dict:
    key = jax.random.key(seed)
    ks = jax.random.split(key, 21)

    def nrm(k, shape, scale=1.0):
        return jax.random.normal(k, shape, jnp.float32) * scale

    beta = (8.0 * DEPTH) ** -0.25
    band_rows = min(BAND_PAST, PAST_LEN)
    return {
        'x_prompt': nrm(ks[0], (BATCH, SEQ, D_MODEL)),
        'x_sample': nrm(ks[1], (DEC_BATCH, DEC_SEQ, D_MODEL)),
        'cache_conv': nrm(ks[2], (DEPTH, DEC_BATCH, CONV_WIDTH - 1, W_CONV), 0.5),
        'state_ret': nrm(ks[3], (DEPTH, DEC_BATCH, N_RET_HEADS, HEAD_DIM, HEAD_DIM), 2.0),
        'cache_fox_k': nrm(ks[4], (DEPTH, DEC_BATCH, PAST_LEN, N_FOX_HEADS, HEAD_DIM)),
        'cache_fox_v': nrm(ks[5], (DEPTH, DEC_BATCH, PAST_LEN, N_FOX_HEADS, HEAD_DIM)),
        'cache_fox_logf': jax.nn.log_sigmoid(nrm(ks[6], (DEPTH, DEC_BATCH, PAST_LEN, N_FOX_HEADS)) + 2.0),
        'cache_band_k': nrm(ks[7], (DEPTH, DEC_BATCH, band_rows, N_BAND_HEADS, HEAD_DIM)),
        'cache_band_v': nrm(ks[8], (DEPTH, DEC_BATCH, band_rows, N_BAND_HEADS, HEAD_DIM)),
        'w_in': nrm(ks[9], (DEPTH, D_MODEL, D_IN), D_MODEL ** -0.5),
        'conv_w': nrm(ks[10], (DEPTH, CONV_WIDTH, W_CONV), CONV_WIDTH ** -0.5),
        'conv_b': nrm(ks[11], (DEPTH, W_CONV), 0.02),
        'conv_ln_g': 1.0 + nrm(ks[12], (DEPTH, W_CONV), 0.02),
        'conv_ln_b': nrm(ks[13], (DEPTH, W_CONV), 0.02),
        'ret_gn_g': 1.0 + nrm(ks[14], (DEPTH, D_RET), 0.02),
        'ret_gn_b': nrm(ks[15], (DEPTH, D_RET), 0.02),
        'fox_bf': 2.0 + nrm(ks[16], (DEPTH, N_FOX_HEADS), 0.1),
        'rel_bias': nrm(ks[17], (DEPTH, N_BAND_HEADS, 2 * REL_CLIP + 1), 0.1),
        'w_out': nrm(ks[18], (DEPTH, D_MIX, D_MODEL), beta * D_MIX ** -0.5),
        'ln_g': 1.0 + nrm(ks[19], (DEPTH, D_MODEL), 0.02),
        'ln_b': nrm(ks[20], (DEPTH, D_MODEL), 0.02),
    }


def reference(x_prompt, x_sample, cache_conv, state_ret, cache_fox_k, cache_fox_v, cache_fox_logf,
              cache_band_k, cache_band_v, w_in, conv_w, conv_b, conv_ln_g, conv_ln_b, ret_gn_g, ret_gn_b,
              fox_bf, rel_bias, w_out, ln_g, ln_b):
    alpha = (2.0 * DEPTH) ** 0.25
    past = cache_fox_k.shape[2]
    xp, xs = x_prompt, x_sample
    st_p, st_s = [], []
    for l in range(DEPTH):
        params = (w_in[l], conv_w[l], conv_b[l], conv_ln_g[l], conv_ln_b[l], ret_gn_g[l], ret_gn_b[l],
                  fox_bf[l], rel_bias[l], w_out[l], ln_g[l], ln_b[l])
        xp, sp = trunk_layer(xp, None, 0, params, alpha)
        hist = (cache_conv[l], state_ret[l], cache_fox_k[l], cache_fox_v[l], cache_fox_logf[l],
                cache_band_k[l], cache_band_v[l])
        xs, ss = trunk_layer(xs, hist, past, params, alpha)
        st_p.append(sp)
        st_s.append(ss)
    return (xp, xs,
            _stack(st_p, 0), _stack(st_p, 1), _stack(st_p, 2), _stack(st_p, 3), _stack(st_p, 4), _stack(st_p, 5), _stack(st_p, 6),
            _stack(st_s, 0), _stack(st_s, 1), _stack(st_s, 2), _stack(st_s, 3), _stack(st_s, 4), _stack(st_s, 5), _stack(st_s, 6))
```

```python
import functools

import numpy as np
import jax
import jax.numpy as jnp
from jax import lax
from jax.experimental import pallas as pl
from jax.experimental.pallas import tpu as pltpu

HEAD_DIM = 128
CHUNK = 64
CONV_WIDTH = 31
BAND_CHUNKS = 8
BAND_PAST = BAND_CHUNKS * CHUNK
REL_CLIP = 128
ROPE_BASE = 10000.0
LN_EPS = 1e-5
SCALE = HEAD_DIM ** -0.5

F32 = jnp.float32
BF16 = jnp.bfloat16
NEG_INF = float("-inf")

VMEM_LIMIT_BYTES = 52 * 1024 * 1024
LANES = 128
CONV_HALO = 32
BAND_QTILE = 2 * CHUNK
BAND_KBLOCKS = (BAND_PAST + BAND_QTILE) // BAND_QTILE
TOEPLITZ_WIDTH = 768

_NT = (((1,), (1,)), ((), ()))
_TN = (((0,), (0,)), ((), ()))


def _params(*sem):
    return pltpu.CompilerParams(dimension_semantics=sem, vmem_limit_bytes=VMEM_LIMIT_BYTES)


def _silu(x):
    return x * jax.nn.sigmoid(x)


def _log_sigmoid(x):
    return jnp.minimum(x, 0.0) - jnp.log1p(jnp.exp(-jnp.abs(x)))


def _inproj_kernel(x_ref, w_ref, o32_ref, o16_ref):
    acc = jnp.dot(x_ref[...], w_ref[...], preferred_element_type=F32)
    o32_ref[...] = acc
    o16_ref[...] = acc.astype(BF16)


def _inproj(x16, w16, tm, tn):
    m, k = x16.shape
    n = w16.shape[1]
    return pl.pallas_call(
        _inproj_kernel,
        grid=(m // tm, n // tn),
        in_specs=[pl.BlockSpec((tm, k), lambda i, j: (i, 0)),
                  pl.BlockSpec((k, tn), lambda i, j: (0, j))],
        out_specs=[pl.BlockSpec((tm, tn), lambda i, j: (i, j)),
                   pl.BlockSpec((tm, tn), lambda i, j: (i, j))],
        out_shape=[jax.ShapeDtypeStruct((m, n), F32), jax.ShapeDtypeStruct((m, n), BF16)],
        compiler_params=_params("parallel", "arbitrary"),
    )(x16, w16)


def _forget_kernel(x_ref, w_ref, o_ref):
    o_ref[...] = jnp.dot(x_ref[...], w_ref[...], preferred_element_type=F32)


def _forget_logits(x16, wf16, tm):
    m, k = x16.shape
    n = wf16.shape[1]
    return pl.pallas_call(
        _forget_kernel,
        grid=(m // tm,),
        in_specs=[pl.BlockSpec((tm, k), lambda i: (i, 0)),
                  pl.BlockSpec((k, n), lambda i: (0, 0))],
        out_specs=pl.BlockSpec((tm, n), lambda i: (i, 0)),
        out_shape=jax.ShapeDtypeStruct((m, n), F32),
        compiler_params=_params("parallel"),
    )(x16, wf16)


def _cumsum_kernel(f_ref, b_ref, init_ref, lf_ref, c_ref, carry_ref, *, apply_log_sigmoid):
    @pl.when(pl.program_id(1) == 0)
    def _():
        carry_ref[...] = init_ref[0]

    x = f_ref[...]
    if apply_log_sigmoid:
        x = _log_sigmoid(x + b_ref[...])
    tt = x.shape[0]
    row = lax.broadcasted_iota(jnp.int32, (tt, tt), 0)
    col = lax.broadcasted_iota(jnp.int32, (tt, tt), 1)
    tri = (col <= row).astype(F32)
    cs = jnp.dot(tri, x, precision=lax.Precision.HIGHEST, preferred_element_type=F32) + carry_ref[...]
    lf_ref[...] = x
    c_ref[...] = cs
    carry_ref[...] = cs[tt - 1:tt, :]


def _cumsum(f, bias, init, batch, tt, apply_log_sigmoid):
    rows, n = f.shape
    nt = rows // batch // tt
    return pl.pallas_call(
        functools.partial(_cumsum_kernel, apply_log_sigmoid=apply_log_sigmoid),
        grid=(batch, nt),
        in_specs=[pl.BlockSpec((tt, n), lambda b, j: (b * nt + j, 0)),
                  pl.BlockSpec((1, n), lambda b, j: (0, 0)),
                  pl.BlockSpec((1, 1, n), lambda b, j: (b, 0, 0))],
        out_specs=[pl.BlockSpec((tt, n), lambda b, j: (b * nt + j, 0)),
                   pl.BlockSpec((tt, n), lambda b, j: (b * nt + j, 0))],
        out_shape=[jax.ShapeDtypeStruct((rows, n), F32), jax.ShapeDtypeStruct((rows, n), F32)],
        scratch_shapes=[pltpu.VMEM((1, n), F32)],
        compiler_params=_params("arbitrary", "arbitrary"),
    )(f, bias, init)


def _conv_kernel(val_ref, glu_ref, gate_ref, buf_ref, w_ref, cb_ref, g_ref, be_ref,
                 out_ref, st_ref, u_ref, y_ref, *, tt, nt):
    j = pl.program_id(1)
    chans = val_ref.shape[1]

    @pl.when(j == 0)
    def _():
        u_ref[0:CONV_HALO, :] = buf_ref[0]

    u_ref[CONV_HALO:CONV_HALO + tt, :] = val_ref[...] * jax.nn.sigmoid(glu_ref[...])
    first = CONV_HALO - (CONV_WIDTH - 1)
    rc = 32
    for r in range(tt // rc):
        acc = jnp.broadcast_to(cb_ref[...], (rc, chans))
        for k in range(CONV_WIDTH):
            lo = r * rc + first + k
            acc = acc + u_ref[lo:lo + rc, :] * w_ref[k:k + 1, :]
        y_ref[r * rc:(r + 1) * rc, :] = acc
    y = y_ref[...]
    mu = jnp.mean(y, axis=-1, keepdims=True)
    d = y - mu
    var = jnp.mean(d * d, axis=-1, keepdims=True)
    yn = d * lax.rsqrt(var + LN_EPS) * g_ref[...] + be_ref[...]
    out_ref[...] = (_silu(yn) * _silu(gate_ref[...])).astype(BF16)

    tail = u_ref[tt:tt + CONV_HALO, :]

    @pl.when(j == nt - 1)
    def _():
        st_ref[0] = tail

    u_ref[0:CONV_HALO, :] = tail


def _conv_branch(z32, buf, conv_w, conv_b, ln_g, ln_b, batch, t_len, width):
    tt = min(t_len, 256)
    nt = t_len // tt
    rows = batch * t_len
    buf32 = jnp.pad(buf.astype(F32), ((0, 0), (CONV_HALO - (CONV_WIDTH - 1), 0), (0, 0)))
    w32 = jnp.pad(conv_w, ((0, CONV_HALO - CONV_WIDTH), (0, 0)))
    row_map = lambda c: (lambda b, j: (b * nt + j, c))
    vec = pl.BlockSpec((1, width), lambda b, j: (0, 0))
    out, st = pl.pallas_call(
        functools.partial(_conv_kernel, tt=tt, nt=nt),
        grid=(batch, nt),
        in_specs=[pl.BlockSpec((tt, width), row_map(0)),
                  pl.BlockSpec((tt, width), row_map(1)),
                  pl.BlockSpec((tt, width), row_map(2)),
                  pl.BlockSpec((1, CONV_HALO, width), lambda b, j: (b, 0, 0)),
                  pl.BlockSpec((CONV_HALO, width), lambda b, j: (0, 0)),
                  vec, vec, vec],
        out_specs=[pl.BlockSpec((tt, width), lambda b, j: (b * nt + j, 0)),
                   pl.BlockSpec((1, CONV_HALO, width), lambda b, j: (b, 0, 0))],
        out_shape=[jax.ShapeDtypeStruct((rows, width), BF16),
                   jax.ShapeDtypeStruct((batch, CONV_HALO, width), F32)],
        scratch_shapes=[pltpu.VMEM((CONV_HALO + tt, width), F32), pltpu.VMEM((tt, width), F32)],
        compiler_params=_params("arbitrary", "arbitrary"),
    )(z32, z32, z32, buf32, w32, conv_b[None, :], ln_g[None, :], ln_b[None, :])
    return out, st[:, CONV_HALO - (CONV_WIDTH - 1):, :]


def _ret_kernel(lg_ref, q_ref, k_ref, v_ref, g_ref, cos_ref, sin_ref, gg_ref, gb_ref, s0_ref,
                out_ref, sout_ref, s_ref, dec_ref, *, blk, nblk):
    h = pl.program_id(1)
    n = pl.program_id(2)
    lg = lg_ref[h]

    @pl.when(n == 0)
    def _():
        s_ref[...] = s0_ref[0, 0]
        i = lax.broadcasted_iota(jnp.int32, (blk, blk), 0)
        j = lax.broadcasted_iota(jnp.int32, (blk, blk), 1)
        diff = i - j
        dec_ref[...] = jnp.where(diff >= 0, jnp.exp(jnp.maximum(diff, 0).astype(F32) * lg), 0.0)

    cos = cos_ref[...]
    sin = sin_ref[...]

    def rot(x):
        return x * cos + pltpu.roll(x, HEAD_DIM // 2, 1) * sin

    q = rot(q_ref[...]) * SCALE
    k = rot(k_ref[...])
    v = v_ref[...]
    idx = lax.broadcasted_iota(jnp.int32, (blk, 1), 0).astype(F32)
    q_dec = jnp.exp((idx + 1.0) * lg)
    k_dec = jnp.exp((blk - 1.0 - idx) * lg)
    s_dec = jnp.exp(jnp.full((1, HEAD_DIM), blk, F32) * lg)

    state = s_ref[...]
    inner = lax.dot_general(q.astype(BF16), k.astype(BF16), _NT, preferred_element_type=F32) * dec_ref[...]
    o = (jnp.dot(inner.astype(BF16), v, preferred_element_type=F32)
         + jnp.dot((q * q_dec).astype(BF16), state.astype(BF16), preferred_element_type=F32))
    new_state = s_dec * state + lax.dot_general((k * k_dec).astype(BF16), v, _TN, preferred_element_type=F32)
    s_ref[...] = new_state

    mu = jnp.mean(o, axis=-1, keepdims=True)
    d = o - mu
    var = jnp.mean(d * d, axis=-1, keepdims=True)
    y = d * lax.rsqrt(var + LN_EPS) * gg_ref[...] + gb_ref[...]
    out_ref[...] = (y * _silu(g_ref[...])).astype(BF16)

    @pl.when(n == nblk - 1)
    def _():
        sout_ref[0, 0] = new_state


def _retention(z32, z16, cos2, sin2, gn_g, gn_b, s0, log_gamma, batch, t_len, blk, col0):
    n_h = s0.shape[1]
    nblk = t_len // blk
    rows = batch * t_len
    col = lambda c: (lambda b, h, n: (b * nblk + n, col0 + c * n_h + h))
    tab = pl.BlockSpec((blk, HEAD_DIM), lambda b, h, n: (n, 0))
    vec = pl.BlockSpec((1, HEAD_DIM), lambda b, h, n: (0, h))
    st = pl.BlockSpec((1, 1, HEAD_DIM, HEAD_DIM), lambda b, h, n: (b, h, 0, 0))
    return pl.pallas_call(
        functools.partial(_ret_kernel, blk=blk, nblk=nblk),
        grid=(batch, n_h, nblk),
        in_specs=[pl.BlockSpec(memory_space=pltpu.SMEM),
                  pl.BlockSpec((blk, HEAD_DIM), col(0)),
                  pl.BlockSpec((blk, HEAD_DIM), col(1)),
                  pl.BlockSpec((blk, HEAD_DIM), col(2)),
                  pl.BlockSpec((blk, HEAD_DIM), col(3)),
                  tab, tab, vec, vec, st],
        out_specs=[pl.BlockSpec((blk, HEAD_DIM), lambda b, h, n: (b * nblk + n, h)), st],
        out_shape=[jax.ShapeDtypeStruct((rows, n_h * HEAD_DIM), BF16),
                   jax.ShapeDtypeStruct(s0.shape, F32)],
        scratch_shapes=[pltpu.VMEM((HEAD_DIM, HEAD_DIM), F32), pltpu.VMEM((blk, blk), F32)],
        compiler_params=_params("arbitrary", "arbitrary", "arbitrary"),
    )(log_gamma, z32, z32, z16, z32, cos2, sin2, gn_g[None, :], gn_b[None, :], s0)


def _fox_prompt_kernel(qt_ref, kt_ref, q_ref, k_ref, v_ref, g_ref, cq_ref, ck_ref, out_ref,
                       m_ref, l_ref, acc_ref, *, n_h):
    p = pl.program_id(1)
    qi = qt_ref[p]
    ki = kt_ref[p]
    tq = q_ref.shape[0]

    @pl.when(ki == 0)
    def _():
        m_ref[...] = jnp.full(m_ref.shape, NEG_INF, F32)
        l_ref[...] = jnp.zeros(l_ref.shape, F32)
        acc_ref[...] = jnp.zeros(acc_ref.shape, F32)

    def accumulate(diagonal):
        if diagonal:
            row = lax.broadcasted_iota(jnp.int32, (tq, tq), 0)
            col = lax.broadcasted_iota(jnp.int32, (tq, tq), 1)
            visible = col <= row
        for h in range(n_h):
            sl = slice(h * HEAD_DIM, (h + 1) * HEAD_DIM)
            s = lax.dot_general(q_ref[:, sl], k_ref[:, sl], _NT, preferred_element_type=F32) * SCALE
            s = s + cq_ref[:, h:h + 1] - ck_ref[0, h:h + 1, :]
            if diagonal:
                s = jnp.where(visible, s, NEG_INF)
            m_prev = m_ref[h]
            m_new = jnp.maximum(m_prev, jnp.max(s, axis=-1, keepdims=True))
            alpha = jnp.exp(m_prev - m_new)
            prob = jnp.exp(s - m_new)
            l_ref[h] = alpha * l_ref[h] + jnp.sum(prob, axis=-1, keepdims=True)
            acc_ref[h] = alpha * acc_ref[h] + jnp.dot(prob.astype(BF16), v_ref[:, sl],
                                                      preferred_element_type=F32)
            m_ref[h] = m_new

    @pl.when(ki < qi)
    def _():
        accumulate(False)

    @pl.when(ki == qi)
    def _():
        accumulate(True)
        for h in range(n_h):
            sl = slice(h * HEAD_DIM, (h + 1) * HEAD_DIM)
            o = acc_ref[h] / l_ref[h]
            out_ref[:, sl] = (o * _silu(g_ref[:, sl])).astype(BF16)


def _fox_prompt(z32, z16, c, c_t, batch, t_len, n_h, col0, tq):
    nq = t_len // tq
    width = n_h * HEAD_DIM
    rows = batch * t_len
    pairs = [(a, b) for a in range(nq) for b in range(a + 1)]
    q_tab = jnp.asarray(np.array([a for a, _ in pairs], np.int32))
    k_tab = jnp.asarray(np.array([b for _, b in pairs], np.int32))
    qmap = lambda c_: (lambda b, p, qt, kt: (b * nq + qt[p], c_))
    kmap = lambda c_: (lambda b, p, qt, kt: (b * nq + kt[p], c_))
    grid_spec = pltpu.PrefetchScalarGridSpec(
        num_scalar_prefetch=2,
        grid=(batch, len(pairs)),
        in_specs=[pl.BlockSpec((tq, width), qmap(col0)),
                  pl.BlockSpec((tq, width), kmap(col0 + 1)),
                  pl.BlockSpec((tq, width), kmap(col0 + 2)),
                  pl.BlockSpec((tq, width), qmap(col0 + 3)),
                  pl.BlockSpec((tq, LANES), qmap(0)),
                  pl.BlockSpec((1, n_h, tq), lambda b, p, qt, kt: (b, 0, kt[p]))],
        out_specs=pl.BlockSpec((tq, width), qmap(0)),
        scratch_shapes=[pltpu.VMEM((n_h, tq, 1), F32), pltpu.VMEM((n_h, tq, 1), F32),
                        pltpu.VMEM((n_h, tq, HEAD_DIM), F32)],
    )
    return pl.pallas_call(
        functools.partial(_fox_prompt_kernel, n_h=n_h),
        grid_spec=grid_spec,
        out_shape=jax.ShapeDtypeStruct((rows, width), BF16),
        compiler_params=_params("arbitrary", "arbitrary"),
    )(q_tab, k_tab, z16, z16, z16, z32, c, c_t)


def _fox_sample_kernel(q_ref, kc_ref, vc_ref, kn_ref, vn_ref, g_ref, cq_ref, ckc_ref, ckn_ref, out_ref):
    q = q_ref[...]
    t_q = q.shape[0]
    cq = cq_ref[0, 0]
    s_c = lax.dot_general(q, kc_ref[0].astype(BF16), _NT, preferred_element_type=F32) * SCALE
    s_c = s_c + cq - ckc_ref[0, 0]
    s_n = lax.dot_general(q, kn_ref[...], _NT, preferred_element_type=F32) * SCALE
    s_n = s_n + cq - ckn_ref[0, 0]
    row = lax.broadcasted_iota(jnp.int32, (t_q, t_q), 0)
    col = lax.broadcasted_iota(jnp.int32, (t_q, t_q), 1)
    s_n = jnp.where(col <= row, s_n, NEG_INF)
    m = jnp.maximum(jnp.max(s_c, axis=-1, keepdims=True), jnp.max(s_n, axis=-1, keepdims=True))
    p_c = jnp.exp(s_c - m)
    p_n = jnp.exp(s_n - m)
    denom = jnp.sum(p_c, axis=-1, keepdims=True) + jnp.sum(p_n, axis=-1, keepdims=True)
    o = (jnp.dot(p_c.astype(BF16), vc_ref[0].astype(BF16), preferred_element_type=F32)
         + jnp.dot(p_n.astype(BF16), vn_ref[...], preferred_element_type=F32)) / denom
    out_ref[...] = (o * _silu(g_ref[...])).astype(BF16)


def _fox_sample(z32, z16, k_cache, v_cache, cq, ckc, ckn, batch, t_len, n_h, col0):
    past = k_cache.shape[1]
    rows = batch * t_len
    col = lambda c: (lambda b, h: (b, col0 + c * n_h + h))
    cache = pl.BlockSpec((1, past, HEAD_DIM), lambda b, h: (b, 0, h))
    return pl.pallas_call(
        _fox_sample_kernel,
        grid=(batch, n_h),
        in_specs=[pl.BlockSpec((t_len, HEAD_DIM), col(0)),
                  cache, cache,
                  pl.BlockSpec((t_len, HEAD_DIM), col(1)),
                  pl.BlockSpec((t_len, HEAD_DIM), col(2)),
                  pl.BlockSpec((t_len, HEAD_DIM), col(3)),
                  pl.BlockSpec((1, 1, t_len, 1), lambda b, h: (b, h, 0, 0)),
                  pl.BlockSpec((1, 1, 1, past), lambda b, h: (b, h, 0, 0)),
                  pl.BlockSpec((1, 1, 1, t_len), lambda b, h: (b, h, 0, 0))],
        out_specs=pl.BlockSpec((t_len, HEAD_DIM), lambda b, h: (b, h)),
        out_shape=jax.ShapeDtypeStruct((rows, n_h * HEAD_DIM), BF16),
        compiler_params=_params("parallel", "arbitrary"),
    )(z16, k_cache, v_cache, z16, z16, z32, cq, ckc, ckn)


def _toeplitz_row(table):
    n_heads = table.shape[0]
    used = BAND_PAST + BAND_QTILE
    far = table[:, 2 * REL_CLIP:]
    row = jnp.concatenate([jnp.broadcast_to(far, (n_heads, BAND_PAST - REL_CLIP)), table[:, ::-1]], axis=1)
    assert row.shape[1] >= used and TOEPLITZ_WIDTH >= used + BAND_QTILE - 1
    tail = jnp.broadcast_to(far, (n_heads, TOEPLITZ_WIDTH - used))
    return jnp.concatenate([row[:, :used], tail], axis=1).astype(F32)


def _toeplitz(row, n_rows):
    return pltpu.roll(jnp.broadcast_to(row, (n_rows, row.shape[1])), 0, 1, stride=1, stride_axis=0)


def _band_prompt_kernel(*refs, n_h):
    nkb = BAND_KBLOCKS
    trev_ref, q_ref = refs[0], refs[1]
    k_refs = refs[2:2 + nkb]
    v_refs = refs[2 + nkb:2 + 2 * nkb]
    g_ref, out_ref, bias_ref = refs[2 + 2 * nkb:]
    qi = pl.program_id(1)
    tq = BAND_QTILE
    n_keys = nkb * tq

    @pl.when(jnp.logical_and(pl.program_id(0) == 0, qi == 0))
    def _():
        qc = lax.broadcasted_iota(jnp.int32, (tq, n_keys), 0) // CHUNK
        kc = lax.broadcasted_iota(jnp.int32, (tq, n_keys), 1) // CHUNK
        visible = jnp.logical_and(kc >= qc, kc <= qc + BAND_CHUNKS)
        for h in range(n_h):
            bias = _toeplitz(trev_ref[h:h + 1, :], tq)[:, :n_keys]
            bias_ref[h] = jnp.where(visible, bias, NEG_INF)

    for h in range(n_h):
        sl = slice(h * HEAD_DIM, (h + 1) * HEAD_DIM)
        q = q_ref[:, sl]
        scores = []
        for jb in range(nkb):
            s = lax.dot_general(q, k_refs[jb][:, sl], _NT, preferred_element_type=F32) * SCALE
            s = s + bias_ref[h, :, jb * tq:(jb + 1) * tq]
            scores.append(jnp.where(qi + jb >= nkb - 1, s, NEG_INF))
        m = functools.reduce(jnp.maximum, [jnp.max(s, axis=-1, keepdims=True) for s in scores])
        probs = [jnp.exp(s - m) for s in scores]
        denom = sum(jnp.sum(p, axis=-1, keepdims=True) for p in probs)
        o = sum(jnp.dot(p.astype(BF16), v_refs[jb][:, sl], preferred_element_type=F32)
                for jb, p in enumerate(probs)) / denom
        out_ref[:, sl] = (o * _silu(g_ref[:, sl])).astype(BF16)


def _band_prompt(z32, z16, trev, batch, t_len, n_h, col0):
    tq = BAND_QTILE
    nq = t_len // tq
    nkb = BAND_KBLOCKS
    width = n_h * HEAD_DIM
    rows = batch * t_len
    qmap = lambda c: (lambda b, i: (b * nq + i, c))
    kmap = lambda c, jb: (lambda b, i: (b * nq + jnp.maximum(i + jb - (nkb - 1), 0), c))
    blk = lambda m: pl.BlockSpec((tq, width), m)
    in_specs = ([pl.BlockSpec(trev.shape, lambda b, i: (0, 0)), blk(qmap(col0))]
                + [blk(kmap(col0 + 1, jb)) for jb in range(nkb)]
                + [blk(kmap(col0 + 2, jb)) for jb in range(nkb)]
                + [blk(qmap(col0 + 3))])
    return pl.pallas_call(
        functools.partial(_band_prompt_kernel, n_h=n_h),
        grid=(batch, nq),
        in_specs=in_specs,
        out_specs=blk(qmap(0)),
        out_shape=jax.ShapeDtypeStruct((rows, width), BF16),
        scratch_shapes=[pltpu.VMEM((n_h, tq, nkb * tq), F32)],
        compiler_params=_params("arbitrary", "arbitrary"),
    )(trev, z16, *([z16] * (2 * nkb)), z32)


def _band_sample_kernel(trev_ref, q_ref, kc_ref, vc_ref, kn_ref, vn_ref, g_ref, out_ref, bias_ref,
                        *, n_h, past):
    t_q = q_ref.shape[0]
    n_cache = kc_ref.shape[1]
    width = bias_ref.shape[2]

    @pl.when(pl.program_id(0) == 0)
    def _():
        i = lax.broadcasted_iota(jnp.int32, (t_q, width), 0)
        j = lax.broadcasted_iota(jnp.int32, (t_q, width), 1)
        q_pos = past + i
        k_pos = past - n_cache + j
        qc = q_pos // CHUNK
        kc = k_pos // CHUNK
        visible = jnp.logical_and(jnp.logical_and(k_pos >= 0, j < n_cache + t_q),
                                  jnp.logical_and(kc <= qc, kc >= qc - BAND_CHUNKS))
        for h in range(n_h):
            bias_ref[h] = jnp.where(visible, _toeplitz(trev_ref[h:h + 1, :], t_q), NEG_INF)

    for h in range(n_h):
        sl = slice(h * HEAD_DIM, (h + 1) * HEAD_DIM)
        q = q_ref[:, sl]
        s_c = lax.dot_general(q, kc_ref[0, :, sl].astype(BF16), _NT, preferred_element_type=F32) * SCALE
        s_c = s_c + bias_ref[h, :, 0:n_cache]
        s_n = lax.dot_general(q, kn_ref[:, sl], _NT, preferred_element_type=F32) * SCALE
        s_n = s_n + bias_ref[h, :, n_cache:n_cache + t_q]
        m = jnp.maximum(jnp.max(s_c, axis=-1, keepdims=True), jnp.max(s_n, axis=-1, keepdims=True))
        p_c = jnp.exp(s_c - m)
        p_n = jnp.exp(s_n - m)
        denom = jnp.sum(p_c, axis=-1, keepdims=True) + jnp.sum(p_n, axis=-1, keepdims=True)
        o = (jnp.dot(p_c.astype(BF16), vc_ref[0, :, sl].astype(BF16), preferred_element_type=F32)
             + jnp.dot(p_n.astype(BF16), vn_ref[:, sl], preferred_element_type=F32)) / denom
        out_ref[:, sl] = (o * _silu(g_ref[:, sl])).astype(BF16)


def _band_sample(z32, z16, k_cache, v_cache, trev, batch, t_len, n_h, col0, past):
    width = n_h * HEAD_DIM
    rows = batch * t_len
    n_cache = k_cache.shape[1]
    assert n_cache == BAND_PAST and past >= BAND_PAST and t_len <= BAND_QTILE
    col = lambda c: (lambda b: (b, c))
    cache = pl.BlockSpec((1, n_cache, width), lambda b: (b, 0, 0))
    blk = lambda m: pl.BlockSpec((t_len, width), m)
    return pl.pallas_call(
        functools.partial(_band_sample_kernel, n_h=n_h, past=past),
        grid=(batch,),
        in_specs=[pl.BlockSpec(trev.shape, lambda b: (0, 0)), blk(col(col0)), cache, cache,
                  blk(col(col0 + 1)), blk(col(col0 + 2)), blk(col(col0 + 3))],
        out_specs=blk(col(0)),
        out_shape=jax.ShapeDtypeStruct((rows, width), BF16),
        scratch_shapes=[pltpu.VMEM((n_h, t_len, TOEPLITZ_WIDTH), F32)],
        compiler_params=_params("arbitrary"),
    )(trev, z16, k_cache, v_cache, z16, z16, z32)


def _outproj_kernel(m0, m1, m2, m3, w_ref, x_ref, g_ref, b_ref, y_ref, y16_ref, acc_ref, *, alpha, ksplit):
    k = pl.program_id(1)
    branches = (m0, m1, m2, m3)
    for kk in range(4 * ksplit):
        @pl.when(k == kk)
        def _(kk=kk):
            part = jnp.dot(branches[kk // ksplit][...], w_ref[...], preferred_element_type=F32)
            if kk == 0:
                acc_ref[...] = part
            else:
                acc_ref[...] += part

    @pl.when(k == 4 * ksplit - 1)
    def _():
        r = alpha * x_ref[...] + acc_ref[...]
        mu = jnp.mean(r, axis=-1, keepdims=True)
        d = r - mu
        var = jnp.mean(d * d, axis=-1, keepdims=True)
        y = d * lax.rsqrt(var + LN_EPS) * g_ref[...] + b_ref[...]
        y_ref[...] = y
        y16_ref[...] = y.astype(BF16)


def _outproj(branches, w16, x32, ln_g, ln_b, alpha, tm):
    m, d_model = x32.shape
    width = branches[0].shape[1]
    ksplit = 2
    tk = width // ksplit
    bmap = lambda a: (lambda i, k: (i, jnp.clip(k - a * ksplit, 0, ksplit - 1)))
    row = pl.BlockSpec((tm, d_model), lambda i, k: (i, 0))
    vec = pl.BlockSpec((1, d_model), lambda i, k: (0, 0))
    return pl.pallas_call(
        functools.partial(_outproj_kernel, alpha=alpha, ksplit=ksplit),
        grid=(m // tm, 4 * ksplit),
        in_specs=[pl.BlockSpec((tm, tk), bmap(a)) for a in range(4)]
                 + [pl.BlockSpec((tk, d_model), lambda i, k: (k, 0)), row, vec, vec],
        out_specs=[row, row],
        out_shape=[jax.ShapeDtypeStruct((m, d_model), F32), jax.ShapeDtypeStruct((m, d_model), BF16)],
        scratch_shapes=[pltpu.VMEM((tm, d_model), F32)],
        compiler_params=_params("parallel", "arbitrary"),
    )(*branches, w16, x32, ln_g[None, :], ln_b[None, :])


def _rope_tables(pos0, t_len):
    half = HEAD_DIM // 2
    inv = ROPE_BASE ** (-jnp.arange(half, dtype=F32) / half)
    ang = (pos0 + jnp.arange(t_len)).astype(F32)[:, None] * inv[None, :]
    cos = jnp.cos(ang)
    sin = jnp.sin(ang)
    return jnp.concatenate([cos, cos], axis=1), jnp.concatenate([-sin, sin], axis=1)


def _head_major(c, batch, t_len, n_h):
    return jnp.swapaxes(c.reshape(batch, t_len, -1)[:, :, :n_h], 1, 2)


def _layer(x32, x16, hist, weights, batch, t_len, pos0, alpha):
    (w_main16, w_f16, conv_w, conv_b, conv_ln_g, conv_ln_b, ret_gn_g, ret_gn_b, fox_bf,
     trev, w_out16, ln_g, ln_b, log_gamma) = weights
    width = conv_w.shape[1]
    n_h = width // HEAD_DIM
    rows = batch * t_len
    prompt = hist is None
    tm = min(rows, 512)

    z32, z16 = _inproj(x16, w_main16, tm, 1024)
    f_logits = _forget_logits(x16, w_f16, tm)
    bias_f = jnp.pad(fox_bf, (0, LANES - n_h))[None, :]
    zero_row = jnp.zeros((batch, 1, LANES), F32)

    if prompt:
        conv_buf = jnp.zeros((batch, CONV_WIDTH - 1, width), F32)
        ret_s0 = jnp.zeros((batch, n_h, HEAD_DIM, HEAD_DIM), F32)
        ret_blk = min(t_len, 256)
    else:
        conv_buf, ret_s0, fox_k_c, fox_v_c, fox_lf_c, band_k_c, band_v_c = hist
        ret_blk = t_len

    mix_a, conv_state = _conv_branch(z32, conv_buf, conv_w, conv_b, conv_ln_g, conv_ln_b, batch, t_len, width)

    cos2, sin2 = _rope_tables(pos0, t_len)
    mix_b, ret_state = _retention(z32, z16, cos2, sin2, ret_gn_g, ret_gn_b, ret_s0.astype(F32), log_gamma,
                                  batch, t_len, ret_blk, 3 * n_h)

    if prompt:
        logf, c = _cumsum(f_logits, bias_f, zero_row, batch, min(t_len, 256), True)
        mix_c = _fox_prompt(z32, z16, c, _head_major(c, batch, t_len, n_h), batch, t_len, n_h, 7,
                            min(t_len, 512))
    else:
        past = fox_k_c.shape[1]
        lf_c = jnp.pad(fox_lf_c.astype(F32), ((0, 0), (0, 0), (0, LANES - n_h))).reshape(batch * past, LANES)
        _, c_cache = _cumsum(lf_c, bias_f, zero_row, batch, min(past, 256), False)
        init = c_cache.reshape(batch, past, LANES)[:, past - 1:, :]
        logf, c_new = _cumsum(f_logits, bias_f, init, batch, t_len, True)
        cq = _head_major(c_new, batch, t_len, n_h)[..., None]
        ckc = _head_major(c_cache, batch, past, n_h)[:, :, None, :]
        ckn = _head_major(c_new, batch, t_len, n_h)[:, :, None, :]
        mix_c = _fox_sample(z32, z16, fox_k_c.reshape(batch, past, width), fox_v_c.reshape(batch, past, width),
                            cq, ckc, ckn, batch, t_len, n_h, 7 * n_h)

    if prompt:
        mix_d = _band_prompt(z32, z16, trev, batch, t_len, n_h, 11)
    else:
        n_cache = band_k_c.shape[1]
        mix_d = _band_sample(z32, z16, band_k_c.reshape(batch, n_cache, width),
                             band_v_c.reshape(batch, n_cache, width), trev, batch, t_len, n_h, 11, pos0)

    y32, y16 = _outproj((mix_a, mix_b, mix_c, mix_d), w_out16, x32, ln_g, ln_b, alpha, min(rows, 256))

    def heads(c0):
        return z32[:, c0 * width:(c0 + 1) * width].reshape(batch, t_len, n_h, HEAD_DIM)

    fox_k, fox_v = heads(8), heads(9)
    band_k, band_v = heads(12), heads(13)
    if prompt:
        keep = min(BAND_PAST, t_len)
        band_k, band_v = band_k[:, t_len - keep:], band_v[:, t_len - keep:]
    logf = logf.reshape(batch, t_len, LANES)[:, :, :n_h]
    return y32, y16, (conv_state, ret_state, fox_k, fox_v, logf, band_k, band_v)


def kernel(x_prompt, x_sample, cache_conv, state_ret, cache_fox_k, cache_fox_v, cache_fox_logf,
           cache_band_k, cache_band_v, w_in, conv_w, conv_b, conv_ln_g, conv_ln_b, ret_gn_g, ret_gn_b,
           fox_bf, rel_bias, w_out, ln_g, ln_b):
    depth = w_in.shape[0]
    alpha = (2.0 * depth) ** 0.25
    b_p, t_p, d_model = x_prompt.shape
    b_s, t_s, _ = x_sample.shape
    past = cache_fox_k.shape[2]
    width = conv_w.shape[2]
    n_h = width // HEAD_DIM
    f_col = 11 * width
    log_gamma = jnp.asarray(np.log1p(-np.exp2(-5.0 - np.arange(n_h))), F32)

    xp32 = x_prompt.reshape(b_p * t_p, d_model)
    xs32 = x_sample.reshape(b_s * t_s, d_model)
    xp16, xs16 = xp32.astype(BF16), xs32.astype(BF16)
    st_p, st_s = [], []
    for l in range(depth):
        w_main16 = jnp.concatenate([w_in[l, :, :f_col], w_in[l, :, f_col + n_h:]], axis=1).astype(BF16)
        w_f16 = jnp.pad(w_in[l, :, f_col:f_col + n_h], ((0, 0), (0, LANES - n_h))).astype(BF16)
        weights = (w_main16, w_f16, conv_w[l], conv_b[l], conv_ln_g[l], conv_ln_b[l], ret_gn_g[l], ret_gn_b[l],
                   fox_bf[l], _toeplitz_row(rel_bias[l]), w_out[l].astype(BF16), ln_g[l], ln_b[l], log_gamma)
        xp32, xp16, sp = _layer(xp32, xp16, None, weights, b_p, t_p, 0, alpha)
        hist = (cache_conv[l], state_ret[l], cache_fox_k[l], cache_fox_v[l], cache_fox_logf[l],
                cache_band_k[l], cache_band_v[l])
        xs32, xs16, ss = _layer(xs32, xs16, hist, weights, b_s, t_s, past, alpha)
        st_p.append(sp)
        st_s.append(ss)

    def stack(states, i):
        return jnp.stack([s[i] for s in states], axis=0)

    return (xp32.reshape(b_p, t_p, d_model), xs32.reshape(b_s, t_s, d_model),
            *[stack(st_p, i) for i in range(7)], *[stack(st_s, i) for i in range(7)])
```

```python
import functools

import numpy as np
import jax
import jax.numpy as jnp
from jax import lax
from jax.experimental import pallas as pl
from jax.experimental.pallas import tpu as pltpu

HEAD_DIM = 128
CHUNK = 64
CONV_WIDTH = 31
BAND_CHUNKS = 8
BAND_PAST = BAND_CHUNKS * CHUNK
REL_CLIP = 128
ROPE_BASE = 10000.0
LN_EPS = 1e-5
SCALE = HEAD_DIM ** -0.5
LOG2E = 1.4426950408889634

F32 = jnp.float32
BF16 = jnp.bfloat16
NEG_INF = float("-inf")

VMEM_LIMIT_BYTES = 52 * 1024 * 1024
LANES = 128
CONV_HALO = 32
BAND_QTILE = 2 * CHUNK
BAND_KBLOCKS = (BAND_PAST + BAND_QTILE) // BAND_QTILE
TOEPLITZ_WIDTH = 768
OUTPROJ_NCHUNK = 1024
OUTPROJ_MCHUNK = 128

_NT = (((1,), (1,)), ((), ()))
_TN = (((0,), (0,)), ((), ()))


def _params(*sem):
    return pltpu.CompilerParams(dimension_semantics=sem, vmem_limit_bytes=VMEM_LIMIT_BYTES)


def _silu(x):
    return x * jax.nn.sigmoid(x)


def _log_sigmoid(x):
    return jnp.minimum(x, 0.0) - jnp.log1p(jnp.exp(-jnp.abs(x)))


def _inproj_kernel(x_ref, w_ref, o32_ref, o16_ref):
    acc = jnp.dot(x_ref[...], w_ref[...], preferred_element_type=F32)
    o32_ref[...] = acc
    o16_ref[...] = acc.astype(BF16)


def _inproj(x16, w16, tm, tn):
    m, k = x16.shape
    n = w16.shape[1]
    return pl.pallas_call(
        _inproj_kernel,
        grid=(m // tm, n // tn),
        in_specs=[pl.BlockSpec((tm, k), lambda i, j: (i, 0)),
                  pl.BlockSpec((k, tn), lambda i, j: (0, j))],
        out_specs=[pl.BlockSpec((tm, tn), lambda i, j: (i, j)),
                   pl.BlockSpec((tm, tn), lambda i, j: (i, j))],
        out_shape=[jax.ShapeDtypeStruct((m, n), F32), jax.ShapeDtypeStruct((m, n), BF16)],
        compiler_params=_params("parallel", "arbitrary"),
    )(x16, w16)


def _forget_kernel(x_ref, w_ref, o_ref):
    o_ref[...] = jnp.dot(x_ref[...], w_ref[...], preferred_element_type=F32)


def _forget_logits(x16, wf16, tm):
    m, k = x16.shape
    n = wf16.shape[1]
    return pl.pallas_call(
        _forget_kernel,
        grid=(m // tm,),
        in_specs=[pl.BlockSpec((tm, k), lambda i: (i, 0)),
                  pl.BlockSpec((k, n), lambda i: (0, 0))],
        out_specs=pl.BlockSpec((tm, n), lambda i: (i, 0)),
        out_shape=jax.ShapeDtypeStruct((m, n), F32),
        compiler_params=_params("parallel"),
    )(x16, wf16)


def _cumsum_kernel(f_ref, b_ref, init_ref, lf_ref, c_ref, carry_ref, *, apply_log_sigmoid):
    @pl.when(pl.program_id(1) == 0)
    def _():
        carry_ref[...] = init_ref[0]

    x = f_ref[...]
    if apply_log_sigmoid:
        x = _log_sigmoid(x + b_ref[...])
    tt = x.shape[0]
    row = lax.broadcasted_iota(jnp.int32, (tt, tt), 0)
    col = lax.broadcasted_iota(jnp.int32, (tt, tt), 1)
    tri = (col <= row).astype(F32)
    cs = jnp.dot(tri, x, precision=lax.Precision.HIGHEST, preferred_element_type=F32) + carry_ref[...]
    lf_ref[...] = x
    c_ref[...] = cs
    carry_ref[...] = cs[tt - 1:tt, :]


def _cumsum(f, bias, init, batch, tt, apply_log_sigmoid):
    rows, n = f.shape
    nt = rows // batch // tt
    return pl.pallas_call(
        functools.partial(_cumsum_kernel, apply_log_sigmoid=apply_log_sigmoid),
        grid=(batch, nt),
        in_specs=[pl.BlockSpec((tt, n), lambda b, j: (b * nt + j, 0)),
                  pl.BlockSpec((1, n), lambda b, j: (0, 0)),
                  pl.BlockSpec((1, 1, n), lambda b, j: (b, 0, 0))],
        out_specs=[pl.BlockSpec((tt, n), lambda b, j: (b * nt + j, 0)),
                   pl.BlockSpec((tt, n), lambda b, j: (b * nt + j, 0))],
        out_shape=[jax.ShapeDtypeStruct((rows, n), F32), jax.ShapeDtypeStruct((rows, n), F32)],
        scratch_shapes=[pltpu.VMEM((1, n), F32)],
        compiler_params=_params("arbitrary", "arbitrary"),
    )(f, bias, init)


def _conv_kernel(val_ref, glu_ref, gate_ref, buf_ref, w_ref, cb_ref, g_ref, be_ref,
                 out_ref, st_ref, u_ref, y_ref, *, tt, nt):
    j = pl.program_id(1)
    chans = val_ref.shape[1]

    @pl.when(j == 0)
    def _():
        u_ref[0:CONV_HALO, :] = buf_ref[0]

    u_ref[CONV_HALO:CONV_HALO + tt, :] = val_ref[...] * jax.nn.sigmoid(glu_ref[...])
    first = CONV_HALO - (CONV_WIDTH - 1)
    rc = 32
    for r in range(tt // rc):
        acc = jnp.broadcast_to(cb_ref[...], (rc, chans))
        for k in range(CONV_WIDTH):
            lo = r * rc + first + k
            acc = acc + u_ref[lo:lo + rc, :] * w_ref[k:k + 1, :]
        y_ref[r * rc:(r + 1) * rc, :] = acc
    y = y_ref[...]
    mu = jnp.mean(y, axis=-1, keepdims=True)
    d = y - mu
    var = jnp.mean(d * d, axis=-1, keepdims=True)
    yn = d * lax.rsqrt(var + LN_EPS) * g_ref[...] + be_ref[...]
    out_ref[...] = (_silu(yn) * _silu(gate_ref[...])).astype(BF16)

    tail = u_ref[tt:tt + CONV_HALO, :]

    @pl.when(j == nt - 1)
    def _():
        st_ref[0] = tail

    u_ref[0:CONV_HALO, :] = tail


def _conv_branch(z32, buf, conv_w, conv_b, ln_g, ln_b, batch, t_len, width):
    tt = min(t_len, 256)
    nt = t_len // tt
    rows = batch * t_len
    buf32 = jnp.pad(buf.astype(F32), ((0, 0), (CONV_HALO - (CONV_WIDTH - 1), 0), (0, 0)))
    w32 = jnp.pad(conv_w, ((0, CONV_HALO - CONV_WIDTH), (0, 0)))
    row_map = lambda c: (lambda b, j: (b * nt + j, c))
    vec = pl.BlockSpec((1, width), lambda b, j: (0, 0))
    out, st = pl.pallas_call(
        functools.partial(_conv_kernel, tt=tt, nt=nt),
        grid=(batch, nt),
        in_specs=[pl.BlockSpec((tt, width), row_map(0)),
                  pl.BlockSpec((tt, width), row_map(1)),
                  pl.BlockSpec((tt, width), row_map(2)),
                  pl.BlockSpec((1, CONV_HALO, width), lambda b, j: (b, 0, 0)),
                  pl.BlockSpec((CONV_HALO, width), lambda b, j: (0, 0)),
                  vec, vec, vec],
        out_specs=[pl.BlockSpec((tt, width), lambda b, j: (b * nt + j, 0)),
                   pl.BlockSpec((1, CONV_HALO, width), lambda b, j: (b, 0, 0))],
        out_shape=[jax.ShapeDtypeStruct((rows, width), BF16),
                   jax.ShapeDtypeStruct((batch, CONV_HALO, width), F32)],
        scratch_shapes=[pltpu.VMEM((CONV_HALO + tt, width), F32), pltpu.VMEM((tt, width), F32)],
        compiler_params=_params("arbitrary", "arbitrary"),
    )(z32, z32, z32, buf32, w32, conv_b[None, :], ln_g[None, :], ln_b[None, :])
    return out, st[:, CONV_HALO - (CONV_WIDTH - 1):, :]


def _ret_kernel(lg_ref, q_ref, k_ref, v_ref, g_ref, cos_ref, sin_ref, gg_ref, gb_ref, s0_ref,
                out_ref, sout_ref, s_ref, dec_ref, *, blk, nblk):
    h = pl.program_id(1)
    n = pl.program_id(2)
    lg = lg_ref[h]

    @pl.when(n == 0)
    def _():
        s_ref[...] = s0_ref[0, 0]
        i = lax.broadcasted_iota(jnp.int32, (blk, blk), 0)
        j = lax.broadcasted_iota(jnp.int32, (blk, blk), 1)
        diff = i - j
        dec_ref[...] = jnp.where(diff >= 0, jnp.exp(jnp.maximum(diff, 0).astype(F32) * lg), 0.0)

    cos = cos_ref[...]
    sin = sin_ref[...]

    def rot(x):
        return x * cos + pltpu.roll(x, HEAD_DIM // 2, 1) * sin

    q = rot(q_ref[...]) * SCALE
    k = rot(k_ref[...])
    v = v_ref[...]
    idx = lax.broadcasted_iota(jnp.int32, (blk, 1), 0).astype(F32)
    q_dec = jnp.exp((idx + 1.0) * lg)
    k_dec = jnp.exp((blk - 1.0 - idx) * lg)
    s_dec = jnp.exp(jnp.full((1, HEAD_DIM), blk, F32) * lg)

    state = s_ref[...]
    inner = lax.dot_general(q.astype(BF16), k.astype(BF16), _NT, preferred_element_type=F32) * dec_ref[...]
    o = (jnp.dot(inner.astype(BF16), v, preferred_element_type=F32)
         + jnp.dot((q * q_dec).astype(BF16), state.astype(BF16), preferred_element_type=F32))
    new_state = s_dec * state + lax.dot_general((k * k_dec).astype(BF16), v, _TN, preferred_element_type=F32)
    s_ref[...] = new_state

    mu = jnp.mean(o, axis=-1, keepdims=True)
    d = o - mu
    var = jnp.mean(d * d, axis=-1, keepdims=True)
    y = d * lax.rsqrt(var + LN_EPS) * gg_ref[...] + gb_ref[...]
    out_ref[...] = (y * _silu(g_ref[...])).astype(BF16)

    @pl.when(n == nblk - 1)
    def _():
        sout_ref[0, 0] = new_state


def _retention(z32, z16, cos2, sin2, gn_g, gn_b, s0, log_gamma, batch, t_len, blk, col0):
    n_h = s0.shape[1]
    nblk = t_len // blk
    rows = batch * t_len
    col = lambda c: (lambda b, h, n: (b * nblk + n, col0 + c * n_h + h))
    tab = pl.BlockSpec((blk, HEAD_DIM), lambda b, h, n: (n, 0))
    vec = pl.BlockSpec((1, HEAD_DIM), lambda b, h, n: (0, h))
    st = pl.BlockSpec((1, 1, HEAD_DIM, HEAD_DIM), lambda b, h, n: (b, h, 0, 0))
    return pl.pallas_call(
        functools.partial(_ret_kernel, blk=blk, nblk=nblk),
        grid=(batch, n_h, nblk),
        in_specs=[pl.BlockSpec(memory_space=pltpu.SMEM),
                  pl.BlockSpec((blk, HEAD_DIM), col(0)),
                  pl.BlockSpec((blk, HEAD_DIM), col(1)),
                  pl.BlockSpec((blk, HEAD_DIM), col(2)),
                  pl.BlockSpec((blk, HEAD_DIM), col(3)),
                  tab, tab, vec, vec, st],
        out_specs=[pl.BlockSpec((blk, HEAD_DIM), lambda b, h, n: (b * nblk + n, h)), st],
        out_shape=[jax.ShapeDtypeStruct((rows, n_h * HEAD_DIM), BF16),
                   jax.ShapeDtypeStruct(s0.shape, F32)],
        scratch_shapes=[pltpu.VMEM((HEAD_DIM, HEAD_DIM), F32), pltpu.VMEM((blk, blk), F32)],
        compiler_params=_params("arbitrary", "arbitrary", "arbitrary"),
    )(log_gamma, z32, z32, z16, z32, cos2, sin2, gn_g[None, :], gn_b[None, :], s0)


def _fox_prompt_kernel(qt_ref, kt_ref, q_ref, k_ref, vt_ref, g_ref, cq_ref, ck_ref, out_ref,
                       m_ref, l_ref, acc_ref, *, n_h):
    p = pl.program_id(1)
    qi = qt_ref[p]
    ki = kt_ref[p]
    tq = q_ref.shape[0]

    @pl.when(ki == 0)
    def _():
        m_ref[...] = jnp.full(m_ref.shape, NEG_INF, F32)
        l_ref[...] = jnp.zeros(l_ref.shape, F32)
        acc_ref[...] = jnp.zeros(acc_ref.shape, F32)

    def accumulate(diagonal):
        ck2 = ck_ref[...] * LOG2E
        cq2 = cq_ref[0] * LOG2E
        if diagonal:
            key = lax.broadcasted_iota(jnp.int32, (tq, tq), 0)
            qry = lax.broadcasted_iota(jnp.int32, (tq, tq), 1)
            visible = key <= qry
        for h in range(n_h):
            sl = slice(h * HEAD_DIM, (h + 1) * HEAD_DIM)
            t = lax.dot_general(k_ref[:, sl], q_ref[:, sl], _NT, preferred_element_type=F32) * (SCALE * LOG2E)
            t = t - ck2[:, h:h + 1]
            if diagonal:
                t = jnp.where(visible, t, NEG_INF)
            cq_h = cq2[h:h + 1, :]
            m_prev = m_ref[h]
            m_new = jnp.maximum(m_prev, cq_h + jnp.max(t, axis=0, keepdims=True))
            alpha = jnp.exp2(m_prev - m_new)
            prob = jnp.exp2(t - (m_new - cq_h))
            l_ref[h] = alpha * l_ref[h] + jnp.sum(prob, axis=0, keepdims=True)
            acc_ref[h] = alpha * acc_ref[h] + jnp.dot(vt_ref[0, sl, :], prob.astype(BF16),
                                                      preferred_element_type=F32)
            m_ref[h] = m_new

    @pl.when(ki < qi)
    def _():
        accumulate(False)

    @pl.when(ki == qi)
    def _():
        accumulate(True)
        for h in range(n_h):
            sl = slice(h * HEAD_DIM, (h + 1) * HEAD_DIM)
            o = (acc_ref[h] / l_ref[h]).T
            out_ref[:, sl] = (o * _silu(g_ref[:, sl])).astype(BF16)


def _fox_prompt(z32, z16, c, c_t, batch, t_len, n_h, col0, tq):
    nq = t_len // tq
    width = n_h * HEAD_DIM
    rows = batch * t_len
    v_t = jnp.swapaxes(z16[:, (col0 + 2) * width:(col0 + 3) * width].reshape(batch, t_len, width), 1, 2)
    pairs = [(a, b) for a in range(nq) for b in range(a + 1)]
    q_tab = jnp.asarray(np.array([a for a, _ in pairs], np.int32))
    k_tab = jnp.asarray(np.array([b for _, b in pairs], np.int32))
    qmap = lambda c_: (lambda b, p, qt, kt: (b * nq + qt[p], c_))
    kmap = lambda c_: (lambda b, p, qt, kt: (b * nq + kt[p], c_))
    grid_spec = pltpu.PrefetchScalarGridSpec(
        num_scalar_prefetch=2,
        grid=(batch, len(pairs)),
        in_specs=[pl.BlockSpec((tq, width), qmap(col0)),
                  pl.BlockSpec((tq, width), kmap(col0 + 1)),
                  pl.BlockSpec((1, width, tq), lambda b, p, qt, kt: (b, 0, kt[p])),
                  pl.BlockSpec((tq, width), qmap(col0 + 3)),
                  pl.BlockSpec((1, n_h, tq), lambda b, p, qt, kt: (b, 0, qt[p])),
                  pl.BlockSpec((tq, LANES), kmap(0))],
        out_specs=pl.BlockSpec((tq, width), qmap(0)),
        scratch_shapes=[pltpu.VMEM((n_h, 1, tq), F32), pltpu.VMEM((n_h, 1, tq), F32),
                        pltpu.VMEM((n_h, HEAD_DIM, tq), F32)],
    )
    return pl.pallas_call(
        functools.partial(_fox_prompt_kernel, n_h=n_h),
        grid_spec=grid_spec,
        out_shape=jax.ShapeDtypeStruct((rows, width), BF16),
        compiler_params=_params("arbitrary", "arbitrary"),
    )(q_tab, k_tab, z16, z16, v_t, z32, c_t, c)


def _fox_sample_kernel(q_ref, kc_ref, vc_ref, kn_ref, vn_ref, g_ref, cq_ref, ckc_ref, ckn_ref, out_ref, *, n_h):
    t_q = q_ref.shape[0]
    row = lax.broadcasted_iota(jnp.int32, (t_q, t_q), 0)
    col = lax.broadcasted_iota(jnp.int32, (t_q, t_q), 1)
    causal = col <= row
    for h in range(n_h):
        sl = slice(h * HEAD_DIM, (h + 1) * HEAD_DIM)
        q = q_ref[:, sl]
        cq = cq_ref[0, h]
        s_c = lax.dot_general(q, kc_ref[:, h, :].astype(BF16), _NT, preferred_element_type=F32) * SCALE
        s_c = s_c + cq - ckc_ref[0, h]
        s_n = lax.dot_general(q, kn_ref[:, sl], _NT, preferred_element_type=F32) * SCALE
        s_n = jnp.where(causal, s_n + cq - ckn_ref[0, h], NEG_INF)
        m = jnp.maximum(jnp.max(s_c, axis=-1, keepdims=True), jnp.max(s_n, axis=-1, keepdims=True))
        p_c = jnp.exp(s_c - m)
        p_n = jnp.exp(s_n - m)
        denom = jnp.sum(p_c, axis=-1, keepdims=True) + jnp.sum(p_n, axis=-1, keepdims=True)
        o = (jnp.dot(p_c.astype(BF16), vc_ref[:, h, :].astype(BF16), preferred_element_type=F32)
             + jnp.dot(p_n.astype(BF16), vn_ref[:, sl], preferred_element_type=F32)) / denom
        out_ref[:, sl] = (o * _silu(g_ref[:, sl])).astype(BF16)


def _fox_sample(z32, z16, k_cache, v_cache, layer, cq, ckc, ckn, batch, t_len, n_h, col0):
    past = k_cache.shape[2]
    rows = batch * t_len
    width = n_h * HEAD_DIM
    col = lambda c: (lambda b: (b, c))
    cache = pl.BlockSpec((None, None, past, n_h, HEAD_DIM), lambda b: (layer, b, 0, 0, 0))
    blk = lambda m: pl.BlockSpec((t_len, width), m)
    return pl.pallas_call(
        functools.partial(_fox_sample_kernel, n_h=n_h),
        grid=(batch,),
        in_specs=[blk(col(col0)), cache, cache, blk(col(col0 + 1)), blk(col(col0 + 2)), blk(col(col0 + 3)),
                  pl.BlockSpec((1, n_h, t_len, 1), lambda b: (b, 0, 0, 0)),
                  pl.BlockSpec((1, n_h, 1, past), lambda b: (b, 0, 0, 0)),
                  pl.BlockSpec((1, n_h, 1, t_len), lambda b: (b, 0, 0, 0))],
        out_specs=blk(col(0)),
        out_shape=jax.ShapeDtypeStruct((rows, width), BF16),
        compiler_params=_params("parallel"),
    )(z16, k_cache, v_cache, z16, z16, z32, cq, ckc, ckn)


def _toeplitz_row(table):
    n_heads = table.shape[0]
    used = BAND_PAST + BAND_QTILE
    far = table[:, 2 * REL_CLIP:]
    row = jnp.concatenate([jnp.broadcast_to(far, (n_heads, BAND_PAST - REL_CLIP)), table[:, ::-1]], axis=1)
    assert row.shape[1] >= used and TOEPLITZ_WIDTH >= used + BAND_QTILE - 1
    tail = jnp.broadcast_to(far, (n_heads, TOEPLITZ_WIDTH - used))
    return jnp.concatenate([row[:, :used], tail], axis=1).astype(F32)


def _toeplitz(row, n_rows):
    return pltpu.roll(jnp.broadcast_to(row, (n_rows, row.shape[1])), 0, 1, stride=1, stride_axis=0)


def _band_prompt_kernel(*refs, n_h):
    nkb = BAND_KBLOCKS
    trev_ref, q_ref = refs[0], refs[1]
    k_refs = refs[2:2 + nkb]
    v_refs = refs[2 + nkb:2 + 2 * nkb]
    g_ref, out_ref, bias_ref = refs[2 + 2 * nkb:]
    qi = pl.program_id(1)
    tq = BAND_QTILE
    n_keys = nkb * tq

    @pl.when(jnp.logical_and(pl.program_id(0) == 0, qi == 0))
    def _():
        qc = lax.broadcasted_iota(jnp.int32, (tq, n_keys), 0) // CHUNK
        kc = lax.broadcasted_iota(jnp.int32, (tq, n_keys), 1) // CHUNK
        visible = jnp.logical_and(kc >= qc, kc <= qc + BAND_CHUNKS)
        for h in range(n_h):
            bias = _toeplitz(trev_ref[h:h + 1, :], tq)[:, :n_keys]
            bias_ref[h] = jnp.where(visible, bias, NEG_INF)

    for h in range(n_h):
        sl = slice(h * HEAD_DIM, (h + 1) * HEAD_DIM)
        q = q_ref[:, sl]
        scores = []
        for jb in range(nkb):
            s = lax.dot_general(q, k_refs[jb][:, sl], _NT, preferred_element_type=F32) * SCALE
            s = s + bias_ref[h, :, jb * tq:(jb + 1) * tq]
            scores.append(jnp.where(qi + jb >= nkb - 1, s, NEG_INF))
        m = functools.reduce(jnp.maximum, [jnp.max(s, axis=-1, keepdims=True) for s in scores])
        probs = [jnp.exp(s - m) for s in scores]
        denom = sum(jnp.sum(p, axis=-1, keepdims=True) for p in probs)
        o = sum(jnp.dot(p.astype(BF16), v_refs[jb][:, sl], preferred_element_type=F32)
                for jb, p in enumerate(probs)) / denom
        out_ref[:, sl] = (o * _silu(g_ref[:, sl])).astype(BF16)


def _band_prompt(z32, z16, trev, batch, t_len, n_h, col0):
    tq = BAND_QTILE
    nq = t_len // tq
    nkb = BAND_KBLOCKS
    width = n_h * HEAD_DIM
    rows = batch * t_len
    qmap = lambda c: (lambda b, i: (b * nq + i, c))
    kmap = lambda c, jb: (lambda b, i: (b * nq + jnp.maximum(i + jb - (nkb - 1), 0), c))
    blk = lambda m: pl.BlockSpec((tq, width), m)
    in_specs = ([pl.BlockSpec(trev.shape, lambda b, i: (0, 0)), blk(qmap(col0))]
                + [blk(kmap(col0 + 1, jb)) for jb in range(nkb)]
                + [blk(kmap(col0 + 2, jb)) for jb in range(nkb)]
                + [blk(qmap(col0 + 3))])
    return pl.pallas_call(
        functools.partial(_band_prompt_kernel, n_h=n_h),
        grid=(batch, nq),
        in_specs=in_specs,
        out_specs=blk(qmap(0)),
        out_shape=jax.ShapeDtypeStruct((rows, width), BF16),
        scratch_shapes=[pltpu.VMEM((n_h, tq, nkb * tq), F32)],
        compiler_params=_params("arbitrary", "arbitrary"),
    )(trev, z16, *([z16] * (2 * nkb)), z32)


def _band_sample_kernel(trev_ref, q_ref, kc_ref, vc_ref, kn_ref, vn_ref, g_ref, out_ref, bias_ref,
                        *, n_h, past):
    t_q = q_ref.shape[0]
    n_cache = kc_ref.shape[0]
    width = bias_ref.shape[2]

    @pl.when(pl.program_id(0) == 0)
    def _():
        i = lax.broadcasted_iota(jnp.int32, (t_q, width), 0)
        j = lax.broadcasted_iota(jnp.int32, (t_q, width), 1)
        q_pos = past + i
        k_pos = past - n_cache + j
        qc = q_pos // CHUNK
        kc = k_pos // CHUNK
        visible = jnp.logical_and(jnp.logical_and(k_pos >= 0, j < n_cache + t_q),
                                  jnp.logical_and(kc <= qc, kc >= qc - BAND_CHUNKS))
        for h in range(n_h):
            bias_ref[h] = jnp.where(visible, _toeplitz(trev_ref[h:h + 1, :], t_q), NEG_INF)

    for h in range(n_h):
        sl = slice(h * HEAD_DIM, (h + 1) * HEAD_DIM)
        q = q_ref[:, sl]
        s_c = lax.dot_general(q, kc_ref[:, h, :].astype(BF16), _NT, preferred_element_type=F32) * SCALE
        s_c = s_c + bias_ref[h, :, 0:n_cache]
        s_n = lax.dot_general(q, kn_ref[:, sl], _NT, preferred_element_type=F32) * SCALE
        s_n = s_n + bias_ref[h, :, n_cache:n_cache + t_q]
        m = jnp.maximum(jnp.max(s_c, axis=-1, keepdims=True), jnp.max(s_n, axis=-1, keepdims=True))
        p_c = jnp.exp(s_c - m)
        p_n = jnp.exp(s_n - m)
        denom = jnp.sum(p_c, axis=-1, keepdims=True) + jnp.sum(p_n, axis=-1, keepdims=True)
        o = (jnp.dot(p_c.astype(BF16), vc_ref[:, h, :].astype(BF16), preferred_element_type=F32)
             + jnp.dot(p_n.astype(BF16), vn_ref[:, sl], preferred_element_type=F32)) / denom
        out_ref[:, sl] = (o * _silu(g_ref[:, sl])).astype(BF16)


def _band_sample(z32, z16, k_cache, v_cache, layer, trev, batch, t_len, n_h, col0, past):
    width = n_h * HEAD_DIM
    rows = batch * t_len
    n_cache = k_cache.shape[2]
    assert n_cache == BAND_PAST and past >= BAND_PAST and t_len <= BAND_QTILE
    col = lambda c: (lambda b: (b, c))
    cache = pl.BlockSpec((None, None, n_cache, n_h, HEAD_DIM), lambda b: (layer, b, 0, 0, 0))
    blk = lambda m: pl.BlockSpec((t_len, width), m)
    return pl.pallas_call(
        functools.partial(_band_sample_kernel, n_h=n_h, past=past),
        grid=(batch,),
        in_specs=[pl.BlockSpec(trev.shape, lambda b: (0, 0)), blk(col(col0)), cache, cache,
                  blk(col(col0 + 1)), blk(col(col0 + 2)), blk(col(col0 + 3))],
        out_specs=blk(col(0)),
        out_shape=jax.ShapeDtypeStruct((rows, width), BF16),
        scratch_shapes=[pltpu.VMEM((n_h, t_len, TOEPLITZ_WIDTH), F32)],
        compiler_params=_params("arbitrary"),
    )(trev, z16, k_cache, v_cache, z16, z16, z32)


def _outproj_kernel(m0, m1, m2, m3, w_ref, x_ref, g_ref, b_ref, y_ref, y16_ref, *, alpha, ksplit):
    k = pl.program_id(1)
    branches = (m0, m1, m2, m3)
    for kk in range(4 * ksplit):
        @pl.when(k == kk)
        def _(kk=kk):
            lhs = branches[kk // ksplit][...]
            for c0 in range(0, y_ref.shape[1], OUTPROJ_NCHUNK):
                cols = slice(c0, c0 + OUTPROJ_NCHUNK)
                part = jnp.dot(lhs, w_ref[:, cols], preferred_element_type=F32)
                if kk == 0:
                    y_ref[:, cols] = part
                else:
                    y_ref[:, cols] += part

    @pl.when(k == 4 * ksplit - 1)
    def _():
        for r0 in range(0, y_ref.shape[0], OUTPROJ_MCHUNK):
            rows = slice(r0, min(r0 + OUTPROJ_MCHUNK, y_ref.shape[0]))
            r = alpha * x_ref[rows, :] + y_ref[rows, :]
            mu = jnp.mean(r, axis=-1, keepdims=True)
            d = r - mu
            var = jnp.mean(d * d, axis=-1, keepdims=True)
            y = d * lax.rsqrt(var + LN_EPS) * g_ref[...] + b_ref[...]
            y_ref[rows, :] = y
            y16_ref[rows, :] = y.astype(BF16)


def _outproj(branches, w16, x32, ln_g, ln_b, alpha, tm):
    m, d_model = x32.shape
    width = branches[0].shape[1]
    ksplit = 4
    tk = width // ksplit
    bmap = lambda a: (lambda i, k: (i, jnp.clip(k - a * ksplit, 0, ksplit - 1)))
    row = pl.BlockSpec((tm, d_model), lambda i, k: (i, 0))
    resid = pl.BlockSpec((tm, d_model), lambda i, k: (i, 0), pipeline_mode=pl.Buffered(1))
    vec = pl.BlockSpec((1, d_model), lambda i, k: (0, 0))
    return pl.pallas_call(
        functools.partial(_outproj_kernel, alpha=alpha, ksplit=ksplit),
        grid=(m // tm, 4 * ksplit),
        in_specs=[pl.BlockSpec((tm, tk), bmap(a)) for a in range(4)]
                 + [pl.BlockSpec((tk, d_model), lambda i, k: (k, 0)), resid, vec, vec],
        out_specs=[row, row],
        out_shape=[jax.ShapeDtypeStruct((m, d_model), F32), jax.ShapeDtypeStruct((m, d_model), BF16)],
        compiler_params=_params("parallel", "arbitrary"),
    )(*branches, w16, x32, ln_g[None, :], ln_b[None, :])


def _rope_tables(pos0, t_len):
    half = HEAD_DIM // 2
    inv = ROPE_BASE ** (-jnp.arange(half, dtype=F32) / half)
    ang = (pos0 + jnp.arange(t_len)).astype(F32)[:, None] * inv[None, :]
    cos = jnp.cos(ang)
    sin = jnp.sin(ang)
    return jnp.concatenate([cos, cos], axis=1), jnp.concatenate([-sin, sin], axis=1)


def _head_major(c, batch, t_len, n_h):
    return jnp.swapaxes(c.reshape(batch, t_len, -1)[:, :, :n_h], 1, 2)


def _layer(x32, x16, hist, weights, batch, t_len, pos0, alpha):
    (w_main16, w_f16, conv_w, conv_b, conv_ln_g, conv_ln_b, ret_gn_g, ret_gn_b, fox_bf,
     trev, w_out16, ln_g, ln_b, log_gamma) = weights
    width = conv_w.shape[1]
    n_h = width // HEAD_DIM
    rows = batch * t_len
    prompt = hist is None
    tm = min(rows, 512)

    z32, z16 = _inproj(x16, w_main16, tm, 1024)
    f_logits = _forget_logits(x16, w_f16, tm)
    bias_f = jnp.pad(fox_bf, (0, LANES - n_h))[None, :]
    zero_row = jnp.zeros((batch, 1, LANES), F32)

    if prompt:
        conv_buf = jnp.zeros((batch, CONV_WIDTH - 1, width), F32)
        ret_s0 = jnp.zeros((batch, n_h, HEAD_DIM, HEAD_DIM), F32)
        ret_blk = min(t_len, 256)
    else:
        conv_buf, ret_s0, fox_k_c, fox_v_c, fox_lf_c, band_k_c, band_v_c, layer = hist
        ret_blk = t_len

    mix_a, conv_state = _conv_branch(z32, conv_buf, conv_w, conv_b, conv_ln_g, conv_ln_b, batch, t_len, width)

    cos2, sin2 = _rope_tables(pos0, t_len)
    mix_b, ret_state = _retention(z32, z16, cos2, sin2, ret_gn_g, ret_gn_b, ret_s0.astype(F32), log_gamma,
                                  batch, t_len, ret_blk, 3 * n_h)

    if prompt:
        logf, c = _cumsum(f_logits, bias_f, zero_row, batch, min(t_len, 256), True)
        mix_c = _fox_prompt(z32, z16, c, _head_major(c, batch, t_len, n_h), batch, t_len, n_h, 7,
                            min(t_len, 512))
    else:
        past = fox_k_c.shape[2]
        lf_c = jnp.pad(fox_lf_c.astype(F32), ((0, 0), (0, 0), (0, LANES - n_h))).reshape(batch * past, LANES)
        _, c_cache = _cumsum(lf_c, bias_f, zero_row, batch, min(past, 256), False)
        init = c_cache.reshape(batch, past, LANES)[:, past - 1:, :]
        logf, c_new = _cumsum(f_logits, bias_f, init, batch, t_len, True)
        cq = _head_major(c_new, batch, t_len, n_h)[..., None]
        ckc = _head_major(c_cache, batch, past, n_h)[:, :, None, :]
        ckn = _head_major(c_new, batch, t_len, n_h)[:, :, None, :]
        mix_c = _fox_sample(z32, z16, fox_k_c, fox_v_c, layer, cq, ckc, ckn, batch, t_len, n_h, 7)

    if prompt:
        mix_d = _band_prompt(z32, z16, trev, batch, t_len, n_h, 11)
    else:
        mix_d = _band_sample(z32, z16, band_k_c, band_v_c, layer, trev, batch, t_len, n_h, 11, pos0)

    y32, y16 = _outproj((mix_a, mix_b, mix_c, mix_d), w_out16, x32, ln_g, ln_b, alpha, min(rows, 512))

    def heads(c0):
        return z32[:, c0 * width:(c0 + 1) * width].reshape(batch, t_len, n_h, HEAD_DIM)

    fox_k, fox_v = heads(8), heads(9)
    band_k, band_v = heads(12), heads(13)
    if prompt:
        keep = min(BAND_PAST, t_len)
        band_k, band_v = band_k[:, t_len - keep:], band_v[:, t_len - keep:]
    logf = logf.reshape(batch, t_len, LANES)[:, :, :n_h]
    return y32, y16, (conv_state, ret_state, fox_k, fox_v, logf, band_k, band_v)


def kernel(x_prompt, x_sample, cache_conv, state_ret, cache_fox_k, cache_fox_v, cache_fox_logf,
           cache_band_k, cache_band_v, w_in, conv_w, conv_b, conv_ln_g, conv_ln_b, ret_gn_g, ret_gn_b,
           fox_bf, rel_bias, w_out, ln_g, ln_b):
    depth = w_in.shape[0]
    alpha = (2.0 * depth) ** 0.25
    b_p, t_p, d_model = x_prompt.shape
    b_s, t_s, _ = x_sample.shape
    past = cache_fox_k.shape[2]
    width = conv_w.shape[2]
    n_h = width // HEAD_DIM
    f_col = 11 * width
    log_gamma = jnp.asarray(np.log1p(-np.exp2(-5.0 - np.arange(n_h))), F32)

    xp32 = x_prompt.reshape(b_p * t_p, d_model)
    xs32 = x_sample.reshape(b_s * t_s, d_model)
    xp16, xs16 = xp32.astype(BF16), xs32.astype(BF16)
    st_p, st_s = [], []
    for l in range(depth):
        w_main16 = jnp.concatenate([w_in[l, :, :f_col], w_in[l, :, f_col + n_h:]], axis=1).astype(BF16)
        w_f16 = jnp.pad(w_in[l, :, f_col:f_col + n_h], ((0, 0), (0, LANES - n_h))).astype(BF16)
        weights = (w_main16, w_f16, conv_w[l], conv_b[l], conv_ln_g[l], conv_ln_b[l], ret_gn_g[l], ret_gn_b[l],
                   fox_bf[l], _toeplitz_row(rel_bias[l]), w_out[l].astype(BF16), ln_g[l], ln_b[l], log_gamma)
        xp32, xp16, sp = _layer(xp32, xp16, None, weights, b_p, t_p, 0, alpha)
        hist = (cache_conv[l], state_ret[l], cache_fox_k, cache_fox_v, cache_fox_logf[l],
                cache_band_k, cache_band_v, l)
        xs32, xs16, ss = _layer(xs32, xs16, hist, weights, b_s, t_s, past, alpha)
        st_p.append(sp)
        st_s.append(ss)

    def stack(states, i):
        return jnp.stack([s[i] for s in states], axis=0)

    return (xp32.reshape(b_p, t_p, d_model), xs32.reshape(b_s, t_s, d_model),
            *[stack(st_p, i) for i in range(7)], *[stack(st_s, i) for i in range(7)])
```

```python
import functools

import numpy as np
import jax
import jax.numpy as jnp
from jax import lax
from jax.experimental import pallas as pl
from jax.experimental.pallas import tpu as pltpu

HEAD_DIM = 128
CHUNK = 64
CONV_WIDTH = 31
BAND_CHUNKS = 8
BAND_PAST = BAND_CHUNKS * CHUNK
REL_CLIP = 128
ROPE_BASE = 10000.0
LN_EPS = 1e-5
SCALE = HEAD_DIM ** -0.5
LOG2E = 1.4426950408889634

F32 = jnp.float32
BF16 = jnp.bfloat16
NEG_INF = float("-inf")

VMEM_LIMIT_BYTES = 52 * 1024 * 1024
LANES = 128
CONV_HALO = 32
BAND_QTILE = 2 * CHUNK
BAND_KBLOCKS = (BAND_PAST + BAND_QTILE) // BAND_QTILE
TOEPLITZ_WIDTH = 768
OUTPROJ_TN = 256
OUTPROJ_MCHUNK = 128

_NT = (((1,), (1,)), ((), ()))
_TN = (((0,), (0,)), ((), ()))


def _params(*sem):
    return pltpu.CompilerParams(dimension_semantics=sem, vmem_limit_bytes=VMEM_LIMIT_BYTES)


def _silu(x):
    return x * jax.nn.sigmoid(x)


def _log_sigmoid(x):
    return jnp.minimum(x, 0.0) - jnp.log1p(jnp.exp(-jnp.abs(x)))


def _inproj_kernel(x_ref, w_ref, o32_ref, o16_ref):
    acc = jnp.dot(x_ref[...], w_ref[...], preferred_element_type=F32)
    o32_ref[...] = acc
    o16_ref[...] = acc.astype(BF16)


def _inproj(x16, w16, tm, tn):
    m, k = x16.shape
    n = w16.shape[1]
    return pl.pallas_call(
        _inproj_kernel,
        grid=(m // tm, n // tn),
        in_specs=[pl.BlockSpec((tm, k), lambda i, j: (i, 0)),
                  pl.BlockSpec((k, tn), lambda i, j: (0, j))],
        out_specs=[pl.BlockSpec((tm, tn), lambda i, j: (i, j)),
                   pl.BlockSpec((tm, tn), lambda i, j: (i, j))],
        out_shape=[jax.ShapeDtypeStruct((m, n), F32), jax.ShapeDtypeStruct((m, n), BF16)],
        compiler_params=_params("parallel", "arbitrary"),
    )(x16, w16)


def _forget_kernel(x_ref, w_ref, o_ref):
    o_ref[...] = jnp.dot(x_ref[...], w_ref[...], preferred_element_type=F32)


def _forget_logits(x16, wf16, tm):
    m, k = x16.shape
    n = wf16.shape[1]
    return pl.pallas_call(
        _forget_kernel,
        grid=(m // tm,),
        in_specs=[pl.BlockSpec((tm, k), lambda i: (i, 0)),
                  pl.BlockSpec((k, n), lambda i: (0, 0))],
        out_specs=pl.BlockSpec((tm, n), lambda i: (i, 0)),
        out_shape=jax.ShapeDtypeStruct((m, n), F32),
        compiler_params=_params("parallel"),
    )(x16, wf16)


def _cumsum_kernel(f_ref, b_ref, init_ref, lf_ref, c_ref, carry_ref, *, apply_log_sigmoid):
    @pl.when(pl.program_id(1) == 0)
    def _():
        carry_ref[...] = init_ref[0]

    x = f_ref[...]
    if apply_log_sigmoid:
        x = _log_sigmoid(x + b_ref[...])
    tt = x.shape[0]
    row = lax.broadcasted_iota(jnp.int32, (tt, tt), 0)
    col = lax.broadcasted_iota(jnp.int32, (tt, tt), 1)
    tri = (col <= row).astype(F32)
    cs = jnp.dot(tri, x, precision=lax.Precision.HIGHEST, preferred_element_type=F32) + carry_ref[...]
    lf_ref[...] = x
    c_ref[...] = cs
    carry_ref[...] = cs[tt - 1:tt, :]


def _cumsum(f, bias, init, batch, tt, apply_log_sigmoid):
    rows, n = f.shape
    nt = rows // batch // tt
    return pl.pallas_call(
        functools.partial(_cumsum_kernel, apply_log_sigmoid=apply_log_sigmoid),
        grid=(batch, nt),
        in_specs=[pl.BlockSpec((tt, n), lambda b, j: (b * nt + j, 0)),
                  pl.BlockSpec((1, n), lambda b, j: (0, 0)),
                  pl.BlockSpec((1, 1, n), lambda b, j: (b, 0, 0))],
        out_specs=[pl.BlockSpec((tt, n), lambda b, j: (b * nt + j, 0)),
                   pl.BlockSpec((tt, n), lambda b, j: (b * nt + j, 0))],
        out_shape=[jax.ShapeDtypeStruct((rows, n), F32), jax.ShapeDtypeStruct((rows, n), F32)],
        scratch_shapes=[pltpu.VMEM((1, n), F32)],
        compiler_params=_params("arbitrary", "arbitrary"),
    )(f, bias, init)


def _conv_kernel(val_ref, glu_ref, gate_ref, buf_ref, w_ref, cb_ref, g_ref, be_ref,
                 out_ref, st_ref, u_ref, y_ref, *, tt, nt):
    j = pl.program_id(1)
    chans = val_ref.shape[1]

    @pl.when(j == 0)
    def _():
        u_ref[0:CONV_HALO, :] = buf_ref[0]

    u_ref[CONV_HALO:CONV_HALO + tt, :] = val_ref[...] * jax.nn.sigmoid(glu_ref[...])
    first = CONV_HALO - (CONV_WIDTH - 1)
    rc = 32
    for r in range(tt // rc):
        acc = jnp.broadcast_to(cb_ref[...], (rc, chans))
        for k in range(CONV_WIDTH):
            lo = r * rc + first + k
            acc = acc + u_ref[lo:lo + rc, :] * w_ref[k:k + 1, :]
        y_ref[r * rc:(r + 1) * rc, :] = acc
    y = y_ref[...]
    mu = jnp.mean(y, axis=-1, keepdims=True)
    d = y - mu
    var = jnp.mean(d * d, axis=-1, keepdims=True)
    yn = d * lax.rsqrt(var + LN_EPS) * g_ref[...] + be_ref[...]
    out_ref[...] = (_silu(yn) * _silu(gate_ref[...])).astype(BF16)

    tail = u_ref[tt:tt + CONV_HALO, :]

    @pl.when(j == nt - 1)
    def _():
        st_ref[0] = tail

    u_ref[0:CONV_HALO, :] = tail


def _conv_branch(z32, buf, conv_w, conv_b, ln_g, ln_b, batch, t_len, width):
    tt = min(t_len, 256)
    nt = t_len // tt
    rows = batch * t_len
    buf32 = jnp.pad(buf.astype(F32), ((0, 0), (CONV_HALO - (CONV_WIDTH - 1), 0), (0, 0)))
    w32 = jnp.pad(conv_w, ((0, CONV_HALO - CONV_WIDTH), (0, 0)))
    row_map = lambda c: (lambda b, j: (b * nt + j, c))
    vec = pl.BlockSpec((1, width), lambda b, j: (0, 0))
    out, st = pl.pallas_call(
        functools.partial(_conv_kernel, tt=tt, nt=nt),
        grid=(batch, nt),
        in_specs=[pl.BlockSpec((tt, width), row_map(0)),
                  pl.BlockSpec((tt, width), row_map(1)),
                  pl.BlockSpec((tt, width), row_map(2)),
                  pl.BlockSpec((1, CONV_HALO, width), lambda b, j: (b, 0, 0)),
                  pl.BlockSpec((CONV_HALO, width), lambda b, j: (0, 0)),
                  vec, vec, vec],
        out_specs=[pl.BlockSpec((tt, width), lambda b, j: (b * nt + j, 0)),
                   pl.BlockSpec((1, CONV_HALO, width), lambda b, j: (b, 0, 0))],
        out_shape=[jax.ShapeDtypeStruct((rows, width), BF16),
                   jax.ShapeDtypeStruct((batch, CONV_HALO, width), F32)],
        scratch_shapes=[pltpu.VMEM((CONV_HALO + tt, width), F32), pltpu.VMEM((tt, width), F32)],
        compiler_params=_params("arbitrary", "arbitrary"),
    )(z32, z32, z32, buf32, w32, conv_b[None, :], ln_g[None, :], ln_b[None, :])
    return out, st[:, CONV_HALO - (CONV_WIDTH - 1):, :]


def _ret_kernel(lg_ref, q_ref, k_ref, v_ref, g_ref, cos_ref, sin_ref, gg_ref, gb_ref, s0_ref,
                out_ref, sout_ref, s_ref, dec_ref, *, blk, nblk):
    n = pl.program_id(1)
    n_h = s_ref.shape[0]

    @pl.when(n == 0)
    def _():
        s_ref[...] = s0_ref[0]
        i = lax.broadcasted_iota(jnp.int32, (blk, blk), 0)
        j = lax.broadcasted_iota(jnp.int32, (blk, blk), 1)
        diff = i - j
        for h in range(n_h):
            dec_ref[h] = jnp.where(diff >= 0, jnp.exp(jnp.maximum(diff, 0).astype(F32) * lg_ref[h]), 0.0)

    cos = cos_ref[...]
    sin = sin_ref[...]

    def rot(x):
        return x * cos + pltpu.roll(x, HEAD_DIM // 2, 1) * sin

    idx = lax.broadcasted_iota(jnp.int32, (blk, 1), 0).astype(F32)
    for h in range(n_h):
        sl = slice(h * HEAD_DIM, (h + 1) * HEAD_DIM)
        lg = lg_ref[h]
        q = rot(q_ref[:, sl]) * SCALE
        k = rot(k_ref[:, sl])
        v = v_ref[:, sl]
        q_dec = jnp.exp((idx + 1.0) * lg)
        k_dec = jnp.exp((blk - 1.0 - idx) * lg)
        s_dec = jnp.exp(jnp.full((1, HEAD_DIM), blk, F32) * lg)

        state = s_ref[h]
        inner = lax.dot_general(q.astype(BF16), k.astype(BF16), _NT, preferred_element_type=F32) * dec_ref[h]
        o = (jnp.dot(inner.astype(BF16), v, preferred_element_type=F32)
             + jnp.dot((q * q_dec).astype(BF16), state.astype(BF16), preferred_element_type=F32))
        s_ref[h] = s_dec * state + lax.dot_general((k * k_dec).astype(BF16), v, _TN,
                                                   preferred_element_type=F32)

        mu = jnp.mean(o, axis=-1, keepdims=True)
        d = o - mu
        var = jnp.mean(d * d, axis=-1, keepdims=True)
        y = d * lax.rsqrt(var + LN_EPS) * gg_ref[:, sl] + gb_ref[:, sl]
        out_ref[:, sl] = (y * _silu(g_ref[:, sl])).astype(BF16)

    @pl.when(n == nblk - 1)
    def _():
        sout_ref[0] = s_ref[...]


def _retention(z32, z16, cos2, sin2, gn_g, gn_b, s0, log_gamma, batch, t_len, blk, col0):
    n_h = s0.shape[1]
    width = n_h * HEAD_DIM
    nblk = t_len // blk
    rows = batch * t_len
    col = lambda c: (lambda b, n: (b * nblk + n, c))
    blk_spec = lambda c: pl.BlockSpec((blk, width), col(c))
    tab = pl.BlockSpec((blk, HEAD_DIM), lambda b, n: (n, 0))
    vec = pl.BlockSpec((1, width), lambda b, n: (0, 0))
    st = pl.BlockSpec((1, n_h, HEAD_DIM, HEAD_DIM), lambda b, n: (b, 0, 0, 0))
    return pl.pallas_call(
        functools.partial(_ret_kernel, blk=blk, nblk=nblk),
        grid=(batch, nblk),
        in_specs=[pl.BlockSpec(memory_space=pltpu.SMEM),
                  blk_spec(col0), blk_spec(col0 + 1), blk_spec(col0 + 2), blk_spec(col0 + 3),
                  tab, tab, vec, vec, st],
        out_specs=[blk_spec(0), st],
        out_shape=[jax.ShapeDtypeStruct((rows, width), BF16),
                   jax.ShapeDtypeStruct(s0.shape, F32)],
        scratch_shapes=[pltpu.VMEM((n_h, HEAD_DIM, HEAD_DIM), F32), pltpu.VMEM((n_h, blk, blk), F32)],
        compiler_params=_params("arbitrary", "arbitrary"),
    )(log_gamma, z32, z32, z16, z32, cos2, sin2, gn_g[None, :], gn_b[None, :], s0)


def _fox_prompt_kernel(qt_ref, kt_ref, q_ref, k_ref, vt_ref, g_ref, cq_ref, ck_ref, out_ref,
                       m_ref, l_ref, acc_ref, *, n_h):
    p = pl.program_id(1)
    qi = qt_ref[p]
    ki = kt_ref[p]
    tq = q_ref.shape[0]

    @pl.when(ki == 0)
    def _():
        m_ref[...] = jnp.full(m_ref.shape, NEG_INF, F32)
        l_ref[...] = jnp.zeros(l_ref.shape, F32)
        acc_ref[...] = jnp.zeros(acc_ref.shape, F32)

    def accumulate(diagonal):
        ck2 = ck_ref[...] * LOG2E
        cq2 = cq_ref[0] * LOG2E
        if diagonal:
            key = lax.broadcasted_iota(jnp.int32, (tq, tq), 0)
            qry = lax.broadcasted_iota(jnp.int32, (tq, tq), 1)
            visible = key <= qry
        for h in range(n_h):
            sl = slice(h * HEAD_DIM, (h + 1) * HEAD_DIM)
            t = lax.dot_general(k_ref[:, sl], q_ref[:, sl], _NT, preferred_element_type=F32) * (SCALE * LOG2E)
            t = t - ck2[:, h:h + 1]
            if diagonal:
                t = jnp.where(visible, t, NEG_INF)
            cq_h = cq2[h:h + 1, :]
            m_prev = m_ref[h]
            m_new = jnp.maximum(m_prev, cq_h + jnp.max(t, axis=0, keepdims=True))
            alpha = jnp.exp2(m_prev - m_new)
            prob = jnp.exp2(t - (m_new - cq_h))
            l_ref[h] = alpha * l_ref[h] + jnp.sum(prob, axis=0, keepdims=True)
            acc_ref[h] = alpha * acc_ref[h] + jnp.dot(vt_ref[0, sl, :], prob.astype(BF16),
                                                      preferred_element_type=F32)
            m_ref[h] = m_new

    @pl.when(ki < qi)
    def _():
        accumulate(False)

    @pl.when(ki == qi)
    def _():
        accumulate(True)
        for h in range(n_h):
            sl = slice(h * HEAD_DIM, (h + 1) * HEAD_DIM)
            o = (acc_ref[h] / l_ref[h]).T
            out_ref[:, sl] = (o * _silu(g_ref[:, sl])).astype(BF16)


def _fox_prompt(z32, z16, c, c_t, batch, t_len, n_h, col0, tq):
    nq = t_len // tq
    width = n_h * HEAD_DIM
    rows = batch * t_len
    v_t = jnp.swapaxes(z16[:, (col0 + 2) * width:(col0 + 3) * width].reshape(batch, t_len, width), 1, 2)
    pairs = [(a, b) for a in range(nq) for b in range(a + 1)]
    q_tab = jnp.asarray(np.array([a for a, _ in pairs], np.int32))
    k_tab = jnp.asarray(np.array([b for _, b in pairs], np.int32))
    qmap = lambda c_: (lambda b, p, qt, kt: (b * nq + qt[p], c_))
    kmap = lambda c_: (lambda b, p, qt, kt: (b * nq + kt[p], c_))
    grid_spec = pltpu.PrefetchScalarGridSpec(
        num_scalar_prefetch=2,
        grid=(batch, len(pairs)),
        in_specs=[pl.BlockSpec((tq, width), qmap(col0)),
                  pl.BlockSpec((tq, width), kmap(col0 + 1)),
                  pl.BlockSpec((1, width, tq), lambda b, p, qt, kt: (b, 0, kt[p])),
                  pl.BlockSpec((tq, width), qmap(col0 + 3)),
                  pl.BlockSpec((1, n_h, tq), lambda b, p, qt, kt: (b, 0, qt[p])),
                  pl.BlockSpec((tq, LANES), kmap(0))],
        out_specs=pl.BlockSpec((tq, width), qmap(0)),
        scratch_shapes=[pltpu.VMEM((n_h, 1, tq), F32), pltpu.VMEM((n_h, 1, tq), F32),
                        pltpu.VMEM((n_h, HEAD_DIM, tq), F32)],
    )
    return pl.pallas_call(
        functools.partial(_fox_prompt_kernel, n_h=n_h),
        grid_spec=grid_spec,
        out_shape=jax.ShapeDtypeStruct((rows, width), BF16),
        compiler_params=_params("arbitrary", "arbitrary"),
    )(q_tab, k_tab, z16, z16, v_t, z32, c_t, c)


def _fox_sample_kernel(q_ref, kc_ref, vc_ref, kn_ref, vn_ref, g_ref, cq_ref, ckc_ref, ckn_ref, out_ref, *, n_h):
    t_q = q_ref.shape[0]
    row = lax.broadcasted_iota(jnp.int32, (t_q, t_q), 0)
    col = lax.broadcasted_iota(jnp.int32, (t_q, t_q), 1)
    causal = col <= row
    for h in range(n_h):
        sl = slice(h * HEAD_DIM, (h + 1) * HEAD_DIM)
        q = q_ref[:, sl]
        cq = cq_ref[0, h]
        s_c = lax.dot_general(q, kc_ref[:, h, :].astype(BF16), _NT, preferred_element_type=F32) * SCALE
        s_c = s_c + cq - ckc_ref[0, h]
        s_n = lax.dot_general(q, kn_ref[:, sl], _NT, preferred_element_type=F32) * SCALE
        s_n = jnp.where(causal, s_n + cq - ckn_ref[0, h], NEG_INF)
        m = jnp.maximum(jnp.max(s_c, axis=-1, keepdims=True), jnp.max(s_n, axis=-1, keepdims=True))
        p_c = jnp.exp(s_c - m)
        p_n = jnp.exp(s_n - m)
        denom = jnp.sum(p_c, axis=-1, keepdims=True) + jnp.sum(p_n, axis=-1, keepdims=True)
        o = (jnp.dot(p_c.astype(BF16), vc_ref[:, h, :].astype(BF16), preferred_element_type=F32)
             + jnp.dot(p_n.astype(BF16), vn_ref[:, sl], preferred_element_type=F32)) / denom
        out_ref[:, sl] = (o * _silu(g_ref[:, sl])).astype(BF16)


def _fox_sample(z32, z16, k_cache, v_cache, layer, cq, ckc, ckn, batch, t_len, n_h, col0):
    past = k_cache.shape[2]
    rows = batch * t_len
    width = n_h * HEAD_DIM
    col = lambda c: (lambda b: (b, c))
    cache = pl.BlockSpec((None, None, past, n_h, HEAD_DIM), lambda b: (layer, b, 0, 0, 0))
    blk = lambda m: pl.BlockSpec((t_len, width), m)
    return pl.pallas_call(
        functools.partial(_fox_sample_kernel, n_h=n_h),
        grid=(batch,),
        in_specs=[blk(col(col0)), cache, cache, blk(col(col0 + 1)), blk(col(col0 + 2)), blk(col(col0 + 3)),
                  pl.BlockSpec((1, n_h, t_len, 1), lambda b: (b, 0, 0, 0)),
                  pl.BlockSpec((1, n_h, 1, past), lambda b: (b, 0, 0, 0)),
                  pl.BlockSpec((1, n_h, 1, t_len), lambda b: (b, 0, 0, 0))],
        out_specs=blk(col(0)),
        out_shape=jax.ShapeDtypeStruct((rows, width), BF16),
        compiler_params=_params("parallel"),
    )(z16, k_cache, v_cache, z16, z16, z32, cq, ckc, ckn)


def _toeplitz_row(table):
    n_heads = table.shape[0]
    used = BAND_PAST + BAND_QTILE
    far = table[:, 2 * REL_CLIP:]
    row = jnp.concatenate([jnp.broadcast_to(far, (n_heads, BAND_PAST - REL_CLIP)), table[:, ::-1]], axis=1)
    assert row.shape[1] >= used and TOEPLITZ_WIDTH >= used + BAND_QTILE - 1
    tail = jnp.broadcast_to(far, (n_heads, TOEPLITZ_WIDTH - used))
    return jnp.concatenate([row[:, :used], tail], axis=1).astype(F32)


def _toeplitz(row, n_rows):
    return pltpu.roll(jnp.broadcast_to(row, (n_rows, row.shape[1])), 0, 1, stride=1, stride_axis=0)


def _band_prompt_kernel(*refs, n_h):
    nkb = BAND_KBLOCKS
    trev_ref, q_ref = refs[0], refs[1]
    k_refs = refs[2:2 + nkb]
    v_refs = refs[2 + nkb:2 + 2 * nkb]
    g_ref, out_ref, bias_ref = refs[2 + 2 * nkb:]
    qi = pl.program_id(1)
    tq = BAND_QTILE
    n_keys = nkb * tq

    @pl.when(jnp.logical_and(pl.program_id(0) == 0, qi == 0))
    def _():
        qc = lax.broadcasted_iota(jnp.int32, (tq, n_keys), 0) // CHUNK
        kc = lax.broadcasted_iota(jnp.int32, (tq, n_keys), 1) // CHUNK
        visible = jnp.logical_and(kc >= qc, kc <= qc + BAND_CHUNKS)
        for h in range(n_h):
            bias = _toeplitz(trev_ref[h:h + 1, :], tq)[:, :n_keys]
            bias_ref[h] = jnp.where(visible, bias, NEG_INF).T

    for h in range(n_h):
        sl = slice(h * HEAD_DIM, (h + 1) * HEAD_DIM)
        q = q_ref[:, sl]
        scores = []
        for jb in range(nkb):
            s = lax.dot_general(k_refs[jb][:, sl], q, _NT, preferred_element_type=F32) * SCALE
            s = s + bias_ref[h, jb * tq:(jb + 1) * tq, :]
            scores.append(jnp.where(qi + jb >= nkb - 1, s, NEG_INF))
        m = functools.reduce(jnp.maximum, [jnp.max(s, axis=0, keepdims=True) for s in scores])
        probs = [jnp.exp(s - m) for s in scores]
        denom = sum(jnp.sum(p, axis=0, keepdims=True) for p in probs)
        o_t = sum(jnp.dot(v_refs[jb][0, sl, :], p.astype(BF16), preferred_element_type=F32)
                  for jb, p in enumerate(probs)) / denom
        out_ref[:, sl] = (o_t.T * _silu(g_ref[:, sl])).astype(BF16)


def _band_prompt(z32, z16, trev, batch, t_len, n_h, col0):
    tq = BAND_QTILE
    nq = t_len // tq
    nkb = BAND_KBLOCKS
    width = n_h * HEAD_DIM
    rows = batch * t_len
    qmap = lambda c: (lambda b, i: (b * nq + i, c))
    kmap = lambda c, jb: (lambda b, i: (b * nq + jnp.maximum(i + jb - (nkb - 1), 0), c))
    blk = lambda m: pl.BlockSpec((tq, width), m)
    v_t = jnp.swapaxes(z16[:, (col0 + 2) * width:(col0 + 3) * width].reshape(batch, t_len, width), 1, 2)
    vmap = lambda jb: (lambda b, i: (b, 0, jnp.maximum(i + jb - (nkb - 1), 0)))
    in_specs = ([pl.BlockSpec(trev.shape, lambda b, i: (0, 0)), blk(qmap(col0))]
                + [blk(kmap(col0 + 1, jb)) for jb in range(nkb)]
                + [pl.BlockSpec((1, width, tq), vmap(jb)) for jb in range(nkb)]
                + [blk(qmap(col0 + 3))])
    return pl.pallas_call(
        functools.partial(_band_prompt_kernel, n_h=n_h),
        grid=(batch, nq),
        in_specs=in_specs,
        out_specs=blk(qmap(0)),
        out_shape=jax.ShapeDtypeStruct((rows, width), BF16),
        scratch_shapes=[pltpu.VMEM((n_h, nkb * tq, tq), F32)],
        compiler_params=_params("arbitrary", "arbitrary"),
    )(trev, z16, *([z16] * nkb), *([v_t] * nkb), z32)


def _band_sample_kernel(trev_ref, q_ref, kc_ref, vc_ref, kn_ref, vn_ref, g_ref, out_ref, bias_ref,
                        *, n_h, past):
    t_q = q_ref.shape[0]
    n_cache = kc_ref.shape[0]
    width = bias_ref.shape[2]

    @pl.when(pl.program_id(0) == 0)
    def _():
        i = lax.broadcasted_iota(jnp.int32, (t_q, width), 0)
        j = lax.broadcasted_iota(jnp.int32, (t_q, width), 1)
        q_pos = past + i
        k_pos = past - n_cache + j
        qc = q_pos // CHUNK
        kc = k_pos // CHUNK
        visible = jnp.logical_and(jnp.logical_and(k_pos >= 0, j < n_cache + t_q),
                                  jnp.logical_and(kc <= qc, kc >= qc - BAND_CHUNKS))
        for h in range(n_h):
            bias_ref[h] = jnp.where(visible, _toeplitz(trev_ref[h:h + 1, :], t_q), NEG_INF)

    for h in range(n_h):
        sl = slice(h * HEAD_DIM, (h + 1) * HEAD_DIM)
        q = q_ref[:, sl]
        s_c = lax.dot_general(q, kc_ref[:, h, :].astype(BF16), _NT, preferred_element_type=F32) * SCALE
        s_c = s_c + bias_ref[h, :, 0:n_cache]
        s_n = lax.dot_general(q, kn_ref[:, sl], _NT, preferred_element_type=F32) * SCALE
        s_n = s_n + bias_ref[h, :, n_cache:n_cache + t_q]
        m = jnp.maximum(jnp.max(s_c, axis=-1, keepdims=True), jnp.max(s_n, axis=-1, keepdims=True))
        p_c = jnp.exp(s_c - m)
        p_n = jnp.exp(s_n - m)
        denom = jnp.sum(p_c, axis=-1, keepdims=True) + jnp.sum(p_n, axis=-1, keepdims=True)
        o = (jnp.dot(p_c.astype(BF16), vc_ref[:, h, :].astype(BF16), preferred_element_type=F32)
             + jnp.dot(p_n.astype(BF16), vn_ref[:, sl], preferred_element_type=F32)) / denom
        out_ref[:, sl] = (o * _silu(g_ref[:, sl])).astype(BF16)


def _band_sample(z32, z16, k_cache, v_cache, layer, trev, batch, t_len, n_h, col0, past):
    width = n_h * HEAD_DIM
    rows = batch * t_len
    n_cache = k_cache.shape[2]
    assert n_cache == BAND_PAST and past >= BAND_PAST and t_len <= BAND_QTILE
    col = lambda c: (lambda b: (b, c))
    cache = pl.BlockSpec((None, None, n_cache, n_h, HEAD_DIM), lambda b: (layer, b, 0, 0, 0))
    blk = lambda m: pl.BlockSpec((t_len, width), m)
    return pl.pallas_call(
        functools.partial(_band_sample_kernel, n_h=n_h, past=past),
        grid=(batch,),
        in_specs=[pl.BlockSpec(trev.shape, lambda b: (0, 0)), blk(col(col0)), cache, cache,
                  blk(col(col0 + 1)), blk(col(col0 + 2)), blk(col(col0 + 3))],
        out_specs=blk(col(0)),
        out_shape=jax.ShapeDtypeStruct((rows, width), BF16),
        scratch_shapes=[pltpu.VMEM((n_h, t_len, TOEPLITZ_WIDTH), F32)],
        compiler_params=_params("arbitrary"),
    )(trev, z16, k_cache, v_cache, z16, z16, z32)


def _outproj_kernel(m0, m1, m2, m3, w_ref, x_ref, g_ref, b_ref, y_ref, y16_ref, *, alpha, n_col):
    j = pl.program_id(1)
    branches = (m0, m1, m2, m3)
    width = m0.shape[1]
    tn = w_ref.shape[1]
    for jj in range(n_col):
        @pl.when(j == jj)
        def _(jj=jj):
            acc = jnp.dot(branches[0][...], w_ref[0:width, :], preferred_element_type=F32)
            for a in range(1, 4):
                acc = acc + jnp.dot(branches[a][...], w_ref[a * width:(a + 1) * width, :],
                                    preferred_element_type=F32)
            y_ref[:, jj * tn:(jj + 1) * tn] = acc

    @pl.when(j == n_col - 1)
    def _():
        for r0 in range(0, y_ref.shape[0], OUTPROJ_MCHUNK):
            rows = slice(r0, min(r0 + OUTPROJ_MCHUNK, y_ref.shape[0]))
            r = alpha * x_ref[rows, :] + y_ref[rows, :]
            mu = jnp.mean(r, axis=-1, keepdims=True)
            d = r - mu
            var = jnp.mean(d * d, axis=-1, keepdims=True)
            y = d * lax.rsqrt(var + LN_EPS) * g_ref[...] + b_ref[...]
            y_ref[rows, :] = y
            y16_ref[rows, :] = y.astype(BF16)


def _outproj(branches, w16, x32, ln_g, ln_b, alpha, tm):
    m, d_model = x32.shape
    width = branches[0].shape[1]
    tn = OUTPROJ_TN
    n_col = d_model // tn
    row = pl.BlockSpec((tm, d_model), lambda i, j: (i, 0))
    resid = pl.BlockSpec((tm, d_model), lambda i, j: (i, 0), pipeline_mode=pl.Buffered(1))
    vec = pl.BlockSpec((1, d_model), lambda i, j: (0, 0))
    return pl.pallas_call(
        functools.partial(_outproj_kernel, alpha=alpha, n_col=n_col),
        grid=(m // tm, n_col),
        in_specs=[pl.BlockSpec((tm, width), lambda i, j: (i, 0), pipeline_mode=pl.Buffered(1))
                  for _ in range(4)]
                 + [pl.BlockSpec((4 * width, tn), lambda i, j: (0, j)), resid, vec, vec],
        out_specs=[row, row],
        out_shape=[jax.ShapeDtypeStruct((m, d_model), F32), jax.ShapeDtypeStruct((m, d_model), BF16)],
        compiler_params=_params("parallel", "arbitrary"),
    )(*branches, w16, x32, ln_g[None, :], ln_b[None, :])


def _rope_tables(pos0, t_len):
    half = HEAD_DIM // 2
    inv = ROPE_BASE ** (-jnp.arange(half, dtype=F32) / half)
    ang = (pos0 + jnp.arange(t_len)).astype(F32)[:, None] * inv[None, :]
    cos = jnp.cos(ang)
    sin = jnp.sin(ang)
    return jnp.concatenate([cos, cos], axis=1), jnp.concatenate([-sin, sin], axis=1)


def _head_major(c, batch, t_len, n_h):
    return jnp.swapaxes(c.reshape(batch, t_len, -1)[:, :, :n_h], 1, 2)


def _layer(x32, x16, hist, weights, batch, t_len, pos0, alpha):
    (w_main16, w_f16, conv_w, conv_b, conv_ln_g, conv_ln_b, ret_gn_g, ret_gn_b, fox_bf,
     trev, w_out16, ln_g, ln_b, log_gamma) = weights
    width = conv_w.shape[1]
    n_h = width // HEAD_DIM
    rows = batch * t_len
    prompt = hist is None
    tm = min(rows, 512)

    z32, z16 = _inproj(x16, w_main16, tm, 1024)
    f_logits = _forget_logits(x16, w_f16, tm)
    bias_f = jnp.pad(fox_bf, (0, LANES - n_h))[None, :]
    zero_row = jnp.zeros((batch, 1, LANES), F32)

    if prompt:
        conv_buf = jnp.zeros((batch, CONV_WIDTH - 1, width), F32)
        ret_s0 = jnp.zeros((batch, n_h, HEAD_DIM, HEAD_DIM), F32)
        ret_blk = min(t_len, 256)
    else:
        conv_buf, ret_s0, fox_k_c, fox_v_c, fox_lf_c, band_k_c, band_v_c, layer = hist
        ret_blk = t_len

    mix_a, conv_state = _conv_branch(z32, conv_buf, conv_w, conv_b, conv_ln_g, conv_ln_b, batch, t_len, width)

    cos2, sin2 = _rope_tables(pos0, t_len)
    mix_b, ret_state = _retention(z32, z16, cos2, sin2, ret_gn_g, ret_gn_b, ret_s0.astype(F32), log_gamma,
                                  batch, t_len, ret_blk, 3)

    if prompt:
        logf, c = _cumsum(f_logits, bias_f, zero_row, batch, min(t_len, 256), True)
        mix_c = _fox_prompt(z32, z16, c, _head_major(c, batch, t_len, n_h), batch, t_len, n_h, 7,
                            min(t_len, 512))
    else:
        past = fox_k_c.shape[2]
        lf_c = jnp.pad(fox_lf_c.astype(F32), ((0, 0), (0, 0), (0, LANES - n_h))).reshape(batch * past, LANES)
        _, c_cache = _cumsum(lf_c, bias_f, zero_row, batch, min(past, 256), False)
        init = c_cache.reshape(batch, past, LANES)[:, past - 1:, :]
        logf, c_new = _cumsum(f_logits, bias_f, init, batch, t_len, True)
        cq = _head_major(c_new, batch, t_len, n_h)[..., None]
        ckc = _head_major(c_cache, batch, past, n_h)[:, :, None, :]
        ckn = _head_major(c_new, batch, t_len, n_h)[:, :, None, :]
        mix_c = _fox_sample(z32, z16, fox_k_c, fox_v_c, layer, cq, ckc, ckn, batch, t_len, n_h, 7)

    if prompt:
        mix_d = _band_prompt(z32, z16, trev, batch, t_len, n_h, 11)
    else:
        mix_d = _band_sample(z32, z16, band_k_c, band_v_c, layer, trev, batch, t_len, n_h, 11, pos0)

    y32, y16 = _outproj((mix_a, mix_b, mix_c, mix_d), w_out16, x32, ln_g, ln_b, alpha, min(rows, 512))

    def heads(c0):
        return z32[:, c0 * width:(c0 + 1) * width].reshape(batch, t_len, n_h, HEAD_DIM)

    fox_k, fox_v = heads(8), heads(9)
    band_k, band_v = heads(12), heads(13)
    if prompt:
        keep = min(BAND_PAST, t_len)
        band_k, band_v = band_k[:, t_len - keep:], band_v[:, t_len - keep:]
    logf = logf.reshape(batch, t_len, LANES)[:, :, :n_h]
    return y32, y16, (conv_state, ret_state, fox_k, fox_v, logf, band_k, band_v)


def kernel(x_prompt, x_sample, cache_conv, state_ret, cache_fox_k, cache_fox_v, cache_fox_logf,
           cache_band_k, cache_band_v, w_in, conv_w, conv_b, conv_ln_g, conv_ln_b, ret_gn_g, ret_gn_b,
           fox_bf, rel_bias, w_out, ln_g, ln_b):
    depth = w_in.shape[0]
    alpha = (2.0 * depth) ** 0.25
    b_p, t_p, d_model = x_prompt.shape
    b_s, t_s, _ = x_sample.shape
    past = cache_fox_k.shape[2]
    width = conv_w.shape[2]
    n_h = width // HEAD_DIM
    f_col = 11 * width
    log_gamma = jnp.asarray(np.log1p(-np.exp2(-5.0 - np.arange(n_h))), F32)

    xp32 = x_prompt.reshape(b_p * t_p, d_model)
    xs32 = x_sample.reshape(b_s * t_s, d_model)
    xp16, xs16 = xp32.astype(BF16), xs32.astype(BF16)
    st_p, st_s = [], []
    for l in range(depth):
        w_main16 = jnp.concatenate([w_in[l, :, :f_col], w_in[l, :, f_col + n_h:]], axis=1).astype(BF16)
        w_f16 = jnp.pad(w_in[l, :, f_col:f_col + n_h], ((0, 0), (0, LANES - n_h))).astype(BF16)
        weights = (w_main16, w_f16, conv_w[l], conv_b[l], conv_ln_g[l], conv_ln_b[l], ret_gn_g[l], ret_gn_b[l],
                   fox_bf[l], _toeplitz_row(rel_bias[l]), w_out[l].astype(BF16), ln_g[l], ln_b[l], log_gamma)
        xp32, xp16, sp = _layer(xp32, xp16, None, weights, b_p, t_p, 0, alpha)
        hist = (cache_conv[l], state_ret[l], cache_fox_k, cache_fox_v, cache_fox_logf[l],
                cache_band_k, cache_band_v, l)
        xs32, xs16, ss = _layer(xs32, xs16, hist, weights, b_s, t_s, past, alpha)
        st_p.append(sp)
        st_s.append(ss)

    def stack(states, i):
        return jnp.stack([s[i] for s in states], axis=0)

    return (xp32.reshape(b_p, t_p, d_model), xs32.reshape(b_s, t_s, d_model),
            *[stack(st_p, i) for i in range(7)], *[stack(st_s, i) for i in range(7)])
```

```python
import functools

import numpy as np
import jax
import jax.numpy as jnp
from jax import lax
from jax.experimental import pallas as pl
from jax.experimental.pallas import tpu as pltpu

HEAD_DIM = 128
CHUNK = 64
CONV_WIDTH = 31
BAND_CHUNKS = 8
BAND_PAST = BAND_CHUNKS * CHUNK
REL_CLIP = 128
ROPE_BASE = 10000.0
LN_EPS = 1e-5
SCALE = HEAD_DIM ** -0.5
LOG2E = 1.4426950408889634

F32 = jnp.float32
BF16 = jnp.bfloat16
NEG_INF = float("-inf")

VMEM_LIMIT_BYTES = 52 * 1024 * 1024
LANES = 128
SUBLANES = 8
CONV_HALO = 32
BAND_QTILE = 2 * CHUNK
BAND_KBLOCKS = (BAND_PAST + BAND_QTILE) // BAND_QTILE
TOEPLITZ_WIDTH = 768
OUTPROJ_TN = 1024
OUTPROJ_MCHUNK = 128

_NT = (((1,), (1,)), ((), ()))
_TN = (((0,), (0,)), ((), ()))


def _params(*sem):
    return pltpu.CompilerParams(dimension_semantics=sem, vmem_limit_bytes=VMEM_LIMIT_BYTES)


def _silu(x):
    return x * jax.nn.sigmoid(x)


def _log_sigmoid(x):
    return jnp.minimum(x, 0.0) - jnp.log1p(jnp.exp(-jnp.abs(x)))


def _inproj_kernel(x_ref, w_ref, o32_ref, o16_ref):
    acc = jnp.dot(x_ref[...], w_ref[...], preferred_element_type=F32)
    o32_ref[...] = acc
    o16_ref[...] = acc.astype(BF16)


def _inproj(x16, w16, tm, tn):
    m, k = x16.shape
    n = w16.shape[1]
    return pl.pallas_call(
        _inproj_kernel,
        grid=(m // tm, n // tn),
        in_specs=[pl.BlockSpec((tm, k), lambda i, j: (i, 0)),
                  pl.BlockSpec((k, tn), lambda i, j: (0, j))],
        out_specs=[pl.BlockSpec((tm, tn), lambda i, j: (i, j)),
                   pl.BlockSpec((tm, tn), lambda i, j: (i, j))],
        out_shape=[jax.ShapeDtypeStruct((m, n), F32), jax.ShapeDtypeStruct((m, n), BF16)],
        compiler_params=_params("parallel", "arbitrary"),
    )(x16, w16)


def _forget_kernel(x_ref, w_ref, o_ref):
    o_ref[...] = jnp.dot(x_ref[...], w_ref[...], preferred_element_type=F32)


def _forget_logits(x16, wf16, tm):
    m, k = x16.shape
    n = wf16.shape[1]
    return pl.pallas_call(
        _forget_kernel,
        grid=(m // tm,),
        in_specs=[pl.BlockSpec((tm, k), lambda i: (i, 0)),
                  pl.BlockSpec((k, n), lambda i: (0, 0))],
        out_specs=pl.BlockSpec((tm, n), lambda i: (i, 0)),
        out_shape=jax.ShapeDtypeStruct((m, n), F32),
        compiler_params=_params("parallel"),
    )(x16, wf16)


def _cumsum_kernel(f_ref, b_ref, init_ref, lf_ref, c_ref, carry_ref, *, apply_log_sigmoid):
    @pl.when(pl.program_id(1) == 0)
    def _():
        carry_ref[...] = init_ref[0]

    x = f_ref[...]
    if apply_log_sigmoid:
        x = _log_sigmoid(x + b_ref[...])
    tt = x.shape[0]
    row = lax.broadcasted_iota(jnp.int32, (tt, tt), 0)
    col = lax.broadcasted_iota(jnp.int32, (tt, tt), 1)
    tri = (col <= row).astype(F32)
    cs = jnp.dot(tri, x, precision=lax.Precision.HIGHEST, preferred_element_type=F32) + carry_ref[...]
    lf_ref[...] = x
    c_ref[...] = cs
    carry_ref[...] = cs[tt - 1:tt, :]


def _cumsum(f, bias, init, batch, tt, apply_log_sigmoid):
    rows, n = f.shape
    nt = rows // batch // tt
    return pl.pallas_call(
        functools.partial(_cumsum_kernel, apply_log_sigmoid=apply_log_sigmoid),
        grid=(batch, nt),
        in_specs=[pl.BlockSpec((tt, n), lambda b, j: (b * nt + j, 0)),
                  pl.BlockSpec((1, n), lambda b, j: (0, 0)),
                  pl.BlockSpec((1, 1, n), lambda b, j: (b, 0, 0))],
        out_specs=[pl.BlockSpec((tt, n), lambda b, j: (b * nt + j, 0)),
                   pl.BlockSpec((tt, n), lambda b, j: (b * nt + j, 0))],
        out_shape=[jax.ShapeDtypeStruct((rows, n), F32), jax.ShapeDtypeStruct((rows, n), F32)],
        scratch_shapes=[pltpu.VMEM((1, n), F32)],
        compiler_params=_params("arbitrary", "arbitrary"),
    )(f, bias, init)


def _conv_kernel(val_ref, glu_ref, gate_ref, buf_ref, w_ref, cb_ref, g_ref, be_ref,
                 out_ref, st_ref, u_ref, y_ref, sh_ref, *, tt, nt):
    j = pl.program_id(1)
    chans = val_ref.shape[1]

    @pl.when(j == 0)
    def _():
        u_ref[0:CONV_HALO, :] = buf_ref[0]

    u_ref[CONV_HALO:CONV_HALO + tt, :] = val_ref[...] * jax.nn.sigmoid(glu_ref[...])
    sh_rows = sh_ref.shape[1]
    for b in range(1, SUBLANES):
        sh_ref[b - 1] = u_ref[b:b + sh_rows, :]
    first = CONV_HALO - (CONV_WIDTH - 1)
    rc = 32
    for r in range(tt // rc):
        acc = jnp.broadcast_to(cb_ref[...], (rc, chans))
        for k in range(CONV_WIDTH):
            a, b = divmod(first + k, SUBLANES)
            lo = r * rc + a * SUBLANES
            src = u_ref[lo:lo + rc, :] if b == 0 else sh_ref[b - 1, lo:lo + rc, :]
            acc = acc + src * w_ref[k:k + 1, :]
        y_ref[r * rc:(r + 1) * rc, :] = acc
    y = y_ref[...]
    mu = jnp.mean(y, axis=-1, keepdims=True)
    d = y - mu
    var = jnp.mean(d * d, axis=-1, keepdims=True)
    yn = d * lax.rsqrt(var + LN_EPS) * g_ref[...] + be_ref[...]
    out_ref[...] = (_silu(yn) * _silu(gate_ref[...])).astype(BF16)

    tail = u_ref[tt:tt + CONV_HALO, :]

    @pl.when(j == nt - 1)
    def _():
        st_ref[0] = tail

    u_ref[0:CONV_HALO, :] = tail


def _conv_branch(z32, buf, conv_w, conv_b, ln_g, ln_b, batch, t_len, width):
    tt = min(t_len, 256)
    nt = t_len // tt
    rows = batch * t_len
    buf32 = jnp.pad(buf.astype(F32), ((0, 0), (CONV_HALO - (CONV_WIDTH - 1), 0), (0, 0)))
    w32 = jnp.pad(conv_w, ((0, CONV_HALO - CONV_WIDTH), (0, 0)))
    row_map = lambda c: (lambda b, j: (b * nt + j, c))
    vec = pl.BlockSpec((1, width), lambda b, j: (0, 0))
    out, st = pl.pallas_call(
        functools.partial(_conv_kernel, tt=tt, nt=nt),
        grid=(batch, nt),
        in_specs=[pl.BlockSpec((tt, width), row_map(0)),
                  pl.BlockSpec((tt, width), row_map(1)),
                  pl.BlockSpec((tt, width), row_map(2)),
                  pl.BlockSpec((1, CONV_HALO, width), lambda b, j: (b, 0, 0)),
                  pl.BlockSpec((CONV_HALO, width), lambda b, j: (0, 0)),
                  vec, vec, vec],
        out_specs=[pl.BlockSpec((tt, width), lambda b, j: (b * nt + j, 0)),
                   pl.BlockSpec((1, CONV_HALO, width), lambda b, j: (b, 0, 0))],
        out_shape=[jax.ShapeDtypeStruct((rows, width), BF16),
                   jax.ShapeDtypeStruct((batch, CONV_HALO, width), F32)],
        scratch_shapes=[pltpu.VMEM((CONV_HALO + tt, width), F32), pltpu.VMEM((tt, width), F32),
                        pltpu.VMEM((SUBLANES - 1, CONV_HALO - SUBLANES + tt, width), F32)],
        compiler_params=_params("arbitrary", "arbitrary"),
    )(z32, z32, z32, buf32, w32, conv_b[None, :], ln_g[None, :], ln_b[None, :])
    return out, st[:, CONV_HALO - (CONV_WIDTH - 1):, :]


def _ret_kernel(lg_ref, q_ref, k_ref, v_ref, g_ref, cos_ref, sin_ref, gg_ref, gb_ref, s0_ref,
                out_ref, sout_ref, s_ref, dec_ref, *, blk, nblk):
    n = pl.program_id(1)
    n_h = s_ref.shape[0]

    @pl.when(n == 0)
    def _():
        s_ref[...] = s0_ref[0]
        i = lax.broadcasted_iota(jnp.int32, (blk, blk), 0)
        j = lax.broadcasted_iota(jnp.int32, (blk, blk), 1)
        diff = i - j
        for h in range(n_h):
            dec_ref[h] = jnp.where(diff >= 0, jnp.exp(jnp.maximum(diff, 0).astype(F32) * lg_ref[h]), 0.0)

    cos = cos_ref[...]
    sin = sin_ref[...]

    def rot(x):
        return x * cos + pltpu.roll(x, HEAD_DIM // 2, 1) * sin

    idx = lax.broadcasted_iota(jnp.int32, (blk, 1), 0).astype(F32)
    for h in range(n_h):
        sl = slice(h * HEAD_DIM, (h + 1) * HEAD_DIM)
        lg = lg_ref[h]
        q = rot(q_ref[:, sl]) * SCALE
        k = rot(k_ref[:, sl])
        v = v_ref[:, sl]
        q_dec = jnp.exp((idx + 1.0) * lg)
        k_dec = jnp.exp((blk - 1.0 - idx) * lg)
        s_dec = jnp.exp(jnp.full((1, HEAD_DIM), blk, F32) * lg)

        state = s_ref[h]
        inner = lax.dot_general(q.astype(BF16), k.astype(BF16), _NT, preferred_element_type=F32) * dec_ref[h]
        o = (jnp.dot(inner.astype(BF16), v, preferred_element_type=F32)
             + jnp.dot((q * q_dec).astype(BF16), state.astype(BF16), preferred_element_type=F32))
        s_ref[h] = s_dec * state + lax.dot_general((k * k_dec).astype(BF16), v, _TN,
                                                   preferred_element_type=F32)

        mu = jnp.mean(o, axis=-1, keepdims=True)
        d = o - mu
        var = jnp.mean(d * d, axis=-1, keepdims=True)
        y = d * lax.rsqrt(var + LN_EPS) * gg_ref[:, sl] + gb_ref[:, sl]
        out_ref[:, sl] = (y * _silu(g_ref[:, sl])).astype(BF16)

    @pl.when(n == nblk - 1)
    def _():
        sout_ref[0] = s_ref[...]


def _retention(z32, z16, cos2, sin2, gn_g, gn_b, s0, log_gamma, batch, t_len, blk, col0):
    n_h = s0.shape[1]
    width = n_h * HEAD_DIM
    nblk = t_len // blk
    rows = batch * t_len
    col = lambda c: (lambda b, n: (b * nblk + n, c))
    blk_spec = lambda c: pl.BlockSpec((blk, width), col(c))
    tab = pl.BlockSpec((blk, HEAD_DIM), lambda b, n: (n, 0))
    vec = pl.BlockSpec((1, width), lambda b, n: (0, 0))
    st = pl.BlockSpec((1, n_h, HEAD_DIM, HEAD_DIM), lambda b, n: (b, 0, 0, 0))
    return pl.pallas_call(
        functools.partial(_ret_kernel, blk=blk, nblk=nblk),
        grid=(batch, nblk),
        in_specs=[pl.BlockSpec(memory_space=pltpu.SMEM),
                  blk_spec(col0), blk_spec(col0 + 1), blk_spec(col0 + 2), blk_spec(col0 + 3),
                  tab, tab, vec, vec, st],
        out_specs=[blk_spec(0), st],
        out_shape=[jax.ShapeDtypeStruct((rows, width), BF16),
                   jax.ShapeDtypeStruct(s0.shape, F32)],
        scratch_shapes=[pltpu.VMEM((n_h, HEAD_DIM, HEAD_DIM), F32), pltpu.VMEM((n_h, blk, blk), F32)],
        compiler_params=_params("arbitrary", "arbitrary"),
    )(log_gamma, z32, z32, z16, z32, cos2, sin2, gn_g[None, :], gn_b[None, :], s0)


def _fox_prompt_kernel(qt_ref, kt_ref, q_ref, k_ref, vt_ref, g_ref, cq_ref, ck_ref, out_ref,
                       m_ref, l_ref, acc_ref, *, n_h):
    p = pl.program_id(1)
    qi = qt_ref[p]
    ki = kt_ref[p]
    tq = q_ref.shape[0]

    @pl.when(ki == 0)
    def _():
        m_ref[...] = jnp.full(m_ref.shape, NEG_INF, F32)
        l_ref[...] = jnp.zeros(l_ref.shape, F32)
        acc_ref[...] = jnp.zeros(acc_ref.shape, F32)

    def accumulate(diagonal):
        ck2 = ck_ref[...] * LOG2E
        cq2 = cq_ref[0] * LOG2E
        if diagonal:
            key = lax.broadcasted_iota(jnp.int32, (tq, tq), 0)
            qry = lax.broadcasted_iota(jnp.int32, (tq, tq), 1)
            visible = key <= qry
        for h in range(n_h):
            sl = slice(h * HEAD_DIM, (h + 1) * HEAD_DIM)
            t = lax.dot_general(k_ref[:, sl], q_ref[:, sl], _NT, preferred_element_type=F32) * (SCALE * LOG2E)
            t = t - ck2[:, h:h + 1]
            if diagonal:
                t = jnp.where(visible, t, NEG_INF)
            cq_h = cq2[h:h + 1, :]
            m_prev = m_ref[h]
            m_new = jnp.maximum(m_prev, cq_h + jnp.max(t, axis=0, keepdims=True))
            alpha = jnp.exp2(m_prev - m_new)
            prob = jnp.exp2(t - (m_new - cq_h))
            l_ref[h] = alpha * l_ref[h] + jnp.sum(prob, axis=0, keepdims=True)
            acc_ref[h] = alpha * acc_ref[h] + jnp.dot(vt_ref[0, sl, :], prob.astype(BF16),
                                                      preferred_element_type=F32)
            m_ref[h] = m_new

    @pl.when(ki < qi)
    def _():
        accumulate(False)

    @pl.when(ki == qi)
    def _():
        accumulate(True)
        for h in range(n_h):
            sl = slice(h * HEAD_DIM, (h + 1) * HEAD_DIM)
            o = (acc_ref[h] / l_ref[h]).T
            out_ref[:, sl] = (o * _silu(g_ref[:, sl])).astype(BF16)


def _fox_prompt(z32, z16, c, c_t, batch, t_len, n_h, col0, tq):
    nq = t_len // tq
    width = n_h * HEAD_DIM
    rows = batch * t_len
    v_t = jnp.swapaxes(z16[:, (col0 + 2) * width:(col0 + 3) * width].reshape(batch, t_len, width), 1, 2)
    pairs = [(a, b) for a in range(nq) for b in range(a + 1)]
    q_tab = jnp.asarray(np.array([a for a, _ in pairs], np.int32))
    k_tab = jnp.asarray(np.array([b for _, b in pairs], np.int32))
    qmap = lambda c_: (lambda b, p, qt, kt: (b * nq + qt[p], c_))
    kmap = lambda c_: (lambda b, p, qt, kt: (b * nq + kt[p], c_))
    grid_spec = pltpu.PrefetchScalarGridSpec(
        num_scalar_prefetch=2,
        grid=(batch, len(pairs)),
        in_specs=[pl.BlockSpec((tq, width), qmap(col0)),
                  pl.BlockSpec((tq, width), kmap(col0 + 1)),
                  pl.BlockSpec((1, width, tq), lambda b, p, qt, kt: (b, 0, kt[p])),
                  pl.BlockSpec((tq, width), qmap(col0 + 3)),
                  pl.BlockSpec((1, n_h, tq), lambda b, p, qt, kt: (b, 0, qt[p])),
                  pl.BlockSpec((tq, LANES), kmap(0))],
        out_specs=pl.BlockSpec((tq, width), qmap(0)),
        scratch_shapes=[pltpu.VMEM((n_h, 1, tq), F32), pltpu.VMEM((n_h, 1, tq), F32),
                        pltpu.VMEM((n_h, HEAD_DIM, tq), F32)],
    )
    return pl.pallas_call(
        functools.partial(_fox_prompt_kernel, n_h=n_h),
        grid_spec=grid_spec,
        out_shape=jax.ShapeDtypeStruct((rows, width), BF16),
        compiler_params=_params("arbitrary", "arbitrary"),
    )(q_tab, k_tab, z16, z16, v_t, z32, c_t, c)


def _fox_sample_kernel(q_ref, kc_ref, vc_ref, kn_ref, vn_ref, g_ref, cq_ref, ckc_ref, ckn_ref, out_ref, *, n_h):
    t_q = q_ref.shape[0]
    row = lax.broadcasted_iota(jnp.int32, (t_q, t_q), 0)
    col = lax.broadcasted_iota(jnp.int32, (t_q, t_q), 1)
    causal = col <= row
    for h in range(n_h):
        sl = slice(h * HEAD_DIM, (h + 1) * HEAD_DIM)
        q = q_ref[:, sl]
        cq = cq_ref[0, h]
        s_c = lax.dot_general(q, kc_ref[:, h, :].astype(BF16), _NT, preferred_element_type=F32) * SCALE
        s_c = s_c + cq - ckc_ref[0, h]
        s_n = lax.dot_general(q, kn_ref[:, sl], _NT, preferred_element_type=F32) * SCALE
        s_n = jnp.where(causal, s_n + cq - ckn_ref[0, h], NEG_INF)
        m = jnp.maximum(jnp.max(s_c, axis=-1, keepdims=True), jnp.max(s_n, axis=-1, keepdims=True))
        p_c = jnp.exp(s_c - m)
        p_n = jnp.exp(s_n - m)
        denom = jnp.sum(p_c, axis=-1, keepdims=True) + jnp.sum(p_n, axis=-1, keepdims=True)
        o = (jnp.dot(p_c.astype(BF16), vc_ref[:, h, :].astype(BF16), preferred_element_type=F32)
             + jnp.dot(p_n.astype(BF16), vn_ref[:, sl], preferred_element_type=F32)) / denom
        out_ref[:, sl] = (o * _silu(g_ref[:, sl])).astype(BF16)


def _fox_sample(z32, z16, k_cache, v_cache, layer, cq, ckc, ckn, batch, t_len, n_h, col0):
    past = k_cache.shape[2]
    rows = batch * t_len
    width = n_h * HEAD_DIM
    col = lambda c: (lambda b: (b, c))
    cache = pl.BlockSpec((None, None, past, n_h, HEAD_DIM), lambda b: (layer, b, 0, 0, 0))
    blk = lambda m: pl.BlockSpec((t_len, width), m)
    return pl.pallas_call(
        functools.partial(_fox_sample_kernel, n_h=n_h),
        grid=(batch,),
        in_specs=[blk(col(col0)), cache, cache, blk(col(col0 + 1)), blk(col(col0 + 2)), blk(col(col0 + 3)),
                  pl.BlockSpec((1, n_h, t_len, 1), lambda b: (b, 0, 0, 0)),
                  pl.BlockSpec((1, n_h, 1, past), lambda b: (b, 0, 0, 0)),
                  pl.BlockSpec((1, n_h, 1, t_len), lambda b: (b, 0, 0, 0))],
        out_specs=blk(col(0)),
        out_shape=jax.ShapeDtypeStruct((rows, width), BF16),
        compiler_params=_params("parallel"),
    )(z16, k_cache, v_cache, z16, z16, z32, cq, ckc, ckn)


def _toeplitz_row(table):
    n_heads = table.shape[0]
    used = BAND_PAST + BAND_QTILE
    far = table[:, 2 * REL_CLIP:]
    row = jnp.concatenate([jnp.broadcast_to(far, (n_heads, BAND_PAST - REL_CLIP)), table[:, ::-1]], axis=1)
    assert row.shape[1] >= used and TOEPLITZ_WIDTH >= used + BAND_QTILE - 1
    tail = jnp.broadcast_to(far, (n_heads, TOEPLITZ_WIDTH - used))
    return jnp.concatenate([row[:, :used], tail], axis=1).astype(F32)


def _toeplitz(row, n_rows):
    return pltpu.roll(jnp.broadcast_to(row, (n_rows, row.shape[1])), 0, 1, stride=1, stride_axis=0)


def _band_prompt_kernel(*refs, n_h):
    nkb = BAND_KBLOCKS
    trev_ref, q_ref = refs[0], refs[1]
    k_refs = refs[2:2 + nkb]
    v_refs = refs[2 + nkb:2 + 2 * nkb]
    g_ref, out_ref, bias_ref = refs[2 + 2 * nkb:]
    qi = pl.program_id(1)
    tq = BAND_QTILE
    n_keys = nkb * tq

    @pl.when(jnp.logical_and(pl.program_id(0) == 0, qi == 0))
    def _():
        qc = lax.broadcasted_iota(jnp.int32, (tq, n_keys), 0) // CHUNK
        kc = lax.broadcasted_iota(jnp.int32, (tq, n_keys), 1) // CHUNK
        visible = jnp.logical_and(kc >= qc, kc <= qc + BAND_CHUNKS)
        for h in range(n_h):
            bias = _toeplitz(trev_ref[h:h + 1, :], tq)[:, :n_keys]
            bias_ref[h] = jnp.where(visible, bias, NEG_INF).T

    for h in range(n_h):
        sl = slice(h * HEAD_DIM, (h + 1) * HEAD_DIM)
        q = q_ref[:, sl]
        scores = []
        for jb in range(nkb):
            s = lax.dot_general(k_refs[jb][:, sl], q, _NT, preferred_element_type=F32) * SCALE
            s = s + bias_ref[h, jb * tq:(jb + 1) * tq, :]
            scores.append(jnp.where(qi + jb >= nkb - 1, s, NEG_INF))
        m = functools.reduce(jnp.maximum, [jnp.max(s, axis=0, keepdims=True) for s in scores])
        probs = [jnp.exp(s - m) for s in scores]
        denom = sum(jnp.sum(p, axis=0, keepdims=True) for p in probs)
        o_t = sum(jnp.dot(v_refs[jb][0, sl, :], p.astype(BF16), preferred_element_type=F32)
                  for jb, p in enumerate(probs)) / denom
        out_ref[:, sl] = (o_t.T * _silu(g_ref[:, sl])).astype(BF16)


def _band_prompt(z32, z16, trev, batch, t_len, n_h, col0):
    tq = BAND_QTILE
    nq = t_len // tq
    nkb = BAND_KBLOCKS
    width = n_h * HEAD_DIM
    rows = batch * t_len
    qmap = lambda c: (lambda b, i: (b * nq + i, c))
    kmap = lambda c, jb: (lambda b, i: (b * nq + jnp.maximum(i + jb - (nkb - 1), 0), c))
    blk = lambda m: pl.BlockSpec((tq, width), m)
    v_t = jnp.swapaxes(z16[:, (col0 + 2) * width:(col0 + 3) * width].reshape(batch, t_len, width), 1, 2)
    vmap = lambda jb: (lambda b, i: (b, 0, jnp.maximum(i + jb - (nkb - 1), 0)))
    in_specs = ([pl.BlockSpec(trev.shape, lambda b, i: (0, 0)), blk(qmap(col0))]
                + [blk(kmap(col0 + 1, jb)) for jb in range(nkb)]
                + [pl.BlockSpec((1, width, tq), vmap(jb)) for jb in range(nkb)]
                + [blk(qmap(col0 + 3))])
    return pl.pallas_call(
        functools.partial(_band_prompt_kernel, n_h=n_h),
        grid=(batch, nq),
        in_specs=in_specs,
        out_specs=blk(qmap(0)),
        out_shape=jax.ShapeDtypeStruct((rows, width), BF16),
        scratch_shapes=[pltpu.VMEM((n_h, nkb * tq, tq), F32)],
        compiler_params=_params("arbitrary", "arbitrary"),
    )(trev, z16, *([z16] * nkb), *([v_t] * nkb), z32)


def _band_sample_kernel(trev_ref, q_ref, kc_ref, vc_ref, kn_ref, vn_ref, g_ref, out_ref, bias_ref,
                        *, n_h, past):
    t_q = q_ref.shape[0]
    n_cache = kc_ref.shape[0]
    width = bias_ref.shape[2]

    @pl.when(pl.program_id(0) == 0)
    def _():
        i = lax.broadcasted_iota(jnp.int32, (t_q, width), 0)
        j = lax.broadcasted_iota(jnp.int32, (t_q, width), 1)
        q_pos = past + i
        k_pos = past - n_cache + j
        qc = q_pos // CHUNK
        kc = k_pos // CHUNK
        visible = jnp.logical_and(jnp.logical_and(k_pos >= 0, j < n_cache + t_q),
                                  jnp.logical_and(kc <= qc, kc >= qc - BAND_CHUNKS))
        for h in range(n_h):
            bias_ref[h] = jnp.where(visible, _toeplitz(trev_ref[h:h + 1, :], t_q), NEG_INF)

    for h in range(n_h):
        sl = slice(h * HEAD_DIM, (h + 1) * HEAD_DIM)
        q = q_ref[:, sl]
        s_c = lax.dot_general(q, kc_ref[:, h, :].astype(BF16), _NT, preferred_element_type=F32) * SCALE
        s_c = s_c + bias_ref[h, :, 0:n_cache]
        s_n = lax.dot_general(q, kn_ref[:, sl], _NT, preferred_element_type=F32) * SCALE
        s_n = s_n + bias_ref[h, :, n_cache:n_cache + t_q]
        m = jnp.maximum(jnp.max(s_c, axis=-1, keepdims=True), jnp.max(s_n, axis=-1, keepdims=True))
        p_c = jnp.exp(s_c - m)
        p_n = jnp.exp(s_n - m)
        denom = jnp.sum(p_c, axis=-1, keepdims=True) + jnp.sum(p_n, axis=-1, keepdims=True)
        o = (jnp.dot(p_c.astype(BF16), vc_ref[:, h, :].astype(BF16), preferred_element_type=F32)
             + jnp.dot(p_n.astype(BF16), vn_ref[:, sl], preferred_element_type=F32)) / denom
        out_ref[:, sl] = (o * _silu(g_ref[:, sl])).astype(BF16)


def _band_sample(z32, z16, k_cache, v_cache, layer, trev, batch, t_len, n_h, col0, past):
    width = n_h * HEAD_DIM
    rows = batch * t_len
    n_cache = k_cache.shape[2]
    assert n_cache == BAND_PAST and past >= BAND_PAST and t_len <= BAND_QTILE
    col = lambda c: (lambda b: (b, c))
    cache = pl.BlockSpec((None, None, n_cache, n_h, HEAD_DIM), lambda b: (layer, b, 0, 0, 0))
    blk = lambda m: pl.BlockSpec((t_len, width), m)
    return pl.pallas_call(
        functools.partial(_band_sample_kernel, n_h=n_h, past=past),
        grid=(batch,),
        in_specs=[pl.BlockSpec(trev.shape, lambda b: (0, 0)), blk(col(col0)), cache, cache,
                  blk(col(col0 + 1)), blk(col(col0 + 2)), blk(col(col0 + 3))],
        out_specs=blk(col(0)),
        out_shape=jax.ShapeDtypeStruct((rows, width), BF16),
        scratch_shapes=[pltpu.VMEM((n_h, t_len, TOEPLITZ_WIDTH), F32)],
        compiler_params=_params("arbitrary"),
    )(trev, z16, k_cache, v_cache, z16, z16, z32)


def _outproj_kernel(m0, m1, m2, m3, w_ref, x_ref, g_ref, b_ref, y_ref, y16_ref, r_ref, mu_ref, rs_ref,
                    *, alpha, n_col):
    j = pl.program_id(1)
    branches = (m0, m1, m2, m3)
    width = m0.shape[1]
    tn = w_ref.shape[1]
    for jj in range(n_col):
        @pl.when(j == jj)
        def _(jj=jj):
            acc = alpha * x_ref[...]
            for a in range(4):
                acc = acc + jnp.dot(branches[a][...], w_ref[a * width:(a + 1) * width, :],
                                    preferred_element_type=F32)
            r_ref[:, jj * tn:(jj + 1) * tn] = acc

    @pl.when(j == n_col)
    def _():
        for r0 in range(0, r_ref.shape[0], OUTPROJ_MCHUNK):
            rows = slice(r0, min(r0 + OUTPROJ_MCHUNK, r_ref.shape[0]))
            r = r_ref[rows, :]
            mu = jnp.mean(r, axis=-1, keepdims=True)
            d = r - mu
            mu_ref[rows, :] = mu
            rs_ref[rows, :] = lax.rsqrt(jnp.mean(d * d, axis=-1, keepdims=True) + LN_EPS)

    for jj in range(n_col):
        @pl.when(j == n_col + jj)
        def _(jj=jj):
            cols = slice(jj * tn, (jj + 1) * tn)
            y = (r_ref[:, cols] - mu_ref[...]) * rs_ref[...] * g_ref[:, cols] + b_ref[:, cols]
            y_ref[...] = y
            y16_ref[...] = y.astype(BF16)


def _outproj(branches, w16, x32, ln_g, ln_b, alpha, tm):
    m, d_model = x32.shape
    width = branches[0].shape[1]
    tn = OUTPROJ_TN
    n_col = d_model // tn
    in_col = lambda i, j: (i, jnp.minimum(j, n_col - 1))
    out_col = lambda i, j: (i, jnp.maximum(j - n_col, 0))
    vec = pl.BlockSpec((1, d_model), lambda i, j: (0, 0))
    return pl.pallas_call(
        functools.partial(_outproj_kernel, alpha=alpha, n_col=n_col),
        grid=(m // tm, 2 * n_col),
        in_specs=[pl.BlockSpec((tm, width), lambda i, j: (i, 0), pipeline_mode=pl.Buffered(1))
                  for _ in range(4)]
                 + [pl.BlockSpec((4 * width, tn), lambda i, j: (0, jnp.minimum(j, n_col - 1))),
                    pl.BlockSpec((tm, tn), in_col), vec, vec],
        out_specs=[pl.BlockSpec((tm, tn), out_col), pl.BlockSpec((tm, tn), out_col)],
        out_shape=[jax.ShapeDtypeStruct((m, d_model), F32), jax.ShapeDtypeStruct((m, d_model), BF16)],
        scratch_shapes=[pltpu.VMEM((tm, d_model), F32), pltpu.VMEM((tm, 1), F32), pltpu.VMEM((tm, 1), F32)],
        compiler_params=_params("parallel", "arbitrary"),
    )(*branches, w16, x32, ln_g[None, :], ln_b[None, :])


def _rope_tables(pos0, t_len):
    half = HEAD_DIM // 2
    inv = ROPE_BASE ** (-jnp.arange(half, dtype=F32) / half)
    ang = (pos0 + jnp.arange(t_len)).astype(F32)[:, None] * inv[None, :]
    cos = jnp.cos(ang)
    sin = jnp.sin(ang)
    return jnp.concatenate([cos, cos], axis=1), jnp.concatenate([-sin, sin], axis=1)


def _head_major(c, batch, t_len, n_h):
    return jnp.swapaxes(c.reshape(batch, t_len, -1)[:, :, :n_h], 1, 2)


def _layer(x32, x16, hist, weights, batch, t_len, pos0, alpha):
    (w_main16, w_f16, conv_w, conv_b, conv_ln_g, conv_ln_b, ret_gn_g, ret_gn_b, fox_bf,
     trev, w_out16, ln_g, ln_b, log_gamma) = weights
    width = conv_w.shape[1]
    n_h = width // HEAD_DIM
    rows = batch * t_len
    prompt = hist is None
    tm = min(rows, 512)

    z32, z16 = _inproj(x16, w_main16, tm, 1024)
    f_logits = _forget_logits(x16, w_f16, tm)
    bias_f = jnp.pad(fox_bf, (0, LANES - n_h))[None, :]
    zero_row = jnp.zeros((batch, 1, LANES), F32)

    if prompt:
        conv_buf = jnp.zeros((batch, CONV_WIDTH - 1, width), F32)
        ret_s0 = jnp.zeros((batch, n_h, HEAD_DIM, HEAD_DIM), F32)
        ret_blk = min(t_len, 256)
    else:
        conv_buf, ret_s0, fox_k_c, fox_v_c, fox_lf_c, band_k_c, band_v_c, layer = hist
        ret_blk = t_len

    mix_a, conv_state = _conv_branch(z32, conv_buf, conv_w, conv_b, conv_ln_g, conv_ln_b, batch, t_len, width)

    cos2, sin2 = _rope_tables(pos0, t_len)
    mix_b, ret_state = _retention(z32, z16, cos2, sin2, ret_gn_g, ret_gn_b, ret_s0.astype(F32), log_gamma,
                                  batch, t_len, ret_blk, 3)

    if prompt:
        logf, c = _cumsum(f_logits, bias_f, zero_row, batch, min(t_len, 256), True)
        mix_c = _fox_prompt(z32, z16, c, _head_major(c, batch, t_len, n_h), batch, t_len, n_h, 7,
                            min(t_len, 1024))
    else:
        past = fox_k_c.shape[2]
        lf_c = jnp.pad(fox_lf_c.astype(F32), ((0, 0), (0, 0), (0, LANES - n_h))).reshape(batch * past, LANES)
        _, c_cache = _cumsum(lf_c, bias_f, zero_row, batch, min(past, 256), False)
        init = c_cache.reshape(batch, past, LANES)[:, past - 1:, :]
        logf, c_new = _cumsum(f_logits, bias_f, init, batch, t_len, True)
        cq = _head_major(c_new, batch, t_len, n_h)[..., None]
        ckc = _head_major(c_cache, batch, past, n_h)[:, :, None, :]
        ckn = _head_major(c_new, batch, t_len, n_h)[:, :, None, :]
        mix_c = _fox_sample(z32, z16, fox_k_c, fox_v_c, layer, cq, ckc, ckn, batch, t_len, n_h, 7)

    if prompt:
        mix_d = _band_prompt(z32, z16, trev, batch, t_len, n_h, 11)
    else:
        mix_d = _band_sample(z32, z16, band_k_c, band_v_c, layer, trev, batch, t_len, n_h, 11, pos0)

    y32, y16 = _outproj((mix_a, mix_b, mix_c, mix_d), w_out16, x32, ln_g, ln_b, alpha, min(rows, 512))

    def heads(c0):
        return z32[:, c0 * width:(c0 + 1) * width].reshape(batch, t_len, n_h, HEAD_DIM)

    fox_k, fox_v = heads(8), heads(9)
    band_k, band_v = heads(12), heads(13)
    if prompt:
        keep = min(BAND_PAST, t_len)
        band_k, band_v = band_k[:, t_len - keep:], band_v[:, t_len - keep:]
    logf = logf.reshape(batch, t_len, LANES)[:, :, :n_h]
    return y32, y16, (conv_state, ret_state, fox_k, fox_v, logf, band_k, band_v)


def kernel(x_prompt, x_sample, cache_conv, state_ret, cache_fox_k, cache_fox_v, cache_fox_logf,
           cache_band_k, cache_band_v, w_in, conv_w, conv_b, conv_ln_g, conv_ln_b, ret_gn_g, ret_gn_b,
           fox_bf, rel_bias, w_out, ln_g, ln_b):
    depth = w_in.shape[0]
    alpha = (2.0 * depth) ** 0.25
    b_p, t_p, d_model = x_prompt.shape
    b_s, t_s, _ = x_sample.shape
    past = cache_fox_k.shape[2]
    width = conv_w.shape[2]
    n_h = width // HEAD_DIM
    f_col = 11 * width
    log_gamma = jnp.asarray(np.log1p(-np.exp2(-5.0 - np.arange(n_h))), F32)

    xp32 = x_prompt.reshape(b_p * t_p, d_model)
    xs32 = x_sample.reshape(b_s * t_s, d_model)
    xp16, xs16 = xp32.astype(BF16), xs32.astype(BF16)
    st_p, st_s = [], []
    for l in range(depth):
        w_main16 = jnp.concatenate([w_in[l, :, :f_col], w_in[l, :, f_col + n_h:]], axis=1).astype(BF16)
        w_f16 = jnp.pad(w_in[l, :, f_col:f_col + n_h], ((0, 0), (0, LANES - n_h))).astype(BF16)
        weights = (w_main16, w_f16, conv_w[l], conv_b[l], conv_ln_g[l], conv_ln_b[l], ret_gn_g[l], ret_gn_b[l],
                   fox_bf[l], _toeplitz_row(rel_bias[l]), w_out[l].astype(BF16), ln_g[l], ln_b[l], log_gamma)
        xp32, xp16, sp = _layer(xp32, xp16, None, weights, b_p, t_p, 0, alpha)
        hist = (cache_conv[l], state_ret[l], cache_fox_k, cache_fox_v, cache_fox_logf[l],
                cache_band_k, cache_band_v, l)
        xs32, xs16, ss = _layer(xs32, xs16, hist, weights, b_s, t_s, past, alpha)
        st_p.append(sp)
        st_s.append(ss)

    def stack(states, i):
        return jnp.stack([s[i] for s in states], axis=0)

    return (xp32.reshape(b_p, t_p, d_model), xs32.reshape(b_s, t_s, d_model),
            *[stack(st_p, i) for i in range(7)], *[stack(st_s, i) for i in range(7)])
```

```python
import functools

import numpy as np
import jax
import jax.numpy as jnp
from jax import lax
from jax.experimental import pallas as pl
from jax.experimental.pallas import tpu as pltpu

HEAD_DIM = 128
CHUNK = 64
CONV_WIDTH = 31
BAND_CHUNKS = 8
BAND_PAST = BAND_CHUNKS * CHUNK
REL_CLIP = 128
ROPE_BASE = 10000.0
LN_EPS = 1e-5
SCALE = HEAD_DIM ** -0.5
LOG2E = 1.4426950408889634

F32 = jnp.float32
BF16 = jnp.bfloat16
NEG_INF = float("-inf")

VMEM_LIMIT_BYTES = 52 * 1024 * 1024
LANES = 128
SUBLANES = 8
CONV_HALO = 32
BAND_QTILE = 2 * CHUNK
BAND_KBLOCKS = (BAND_PAST + BAND_QTILE) // BAND_QTILE
TOEPLITZ_WIDTH = 768
OUTPROJ_TN = 1024
OUTPROJ_MCHUNK = 128

_NT = (((1,), (1,)), ((), ()))
_TN = (((0,), (0,)), ((), ()))


def _params(*sem):
    return pltpu.CompilerParams(dimension_semantics=sem, vmem_limit_bytes=VMEM_LIMIT_BYTES)


def _silu(x):
    return x * jax.nn.sigmoid(x)


def _log_sigmoid(x):
    return jnp.minimum(x, 0.0) - jnp.log1p(jnp.exp(-jnp.abs(x)))


def _inproj_kernel(x_ref, w_ref, o32_ref, o16_ref):
    acc = jnp.dot(x_ref[...], w_ref[...], preferred_element_type=F32)
    o32_ref[...] = acc
    o16_ref[...] = acc.astype(BF16)


def _inproj(x16, w16, tm, tn):
    m, k = x16.shape
    n = w16.shape[1]
    return pl.pallas_call(
        _inproj_kernel,
        grid=(m // tm, n // tn),
        in_specs=[pl.BlockSpec((tm, k), lambda i, j: (i, 0)),
                  pl.BlockSpec((k, tn), lambda i, j: (0, j))],
        out_specs=[pl.BlockSpec((tm, tn), lambda i, j: (i, j)),
                   pl.BlockSpec((tm, tn), lambda i, j: (i, j))],
        out_shape=[jax.ShapeDtypeStruct((m, n), F32), jax.ShapeDtypeStruct((m, n), BF16)],
        compiler_params=_params("parallel", "arbitrary"),
    )(x16, w16)


def _forget_kernel(x_ref, w_ref, o_ref):
    o_ref[...] = jnp.dot(x_ref[...], w_ref[...], preferred_element_type=F32)


def _forget_logits(x16, wf16, tm):
    m, k = x16.shape
    n = wf16.shape[1]
    return pl.pallas_call(
        _forget_kernel,
        grid=(m // tm,),
        in_specs=[pl.BlockSpec((tm, k), lambda i: (i, 0)),
                  pl.BlockSpec((k, n), lambda i: (0, 0))],
        out_specs=pl.BlockSpec((tm, n), lambda i: (i, 0)),
        out_shape=jax.ShapeDtypeStruct((m, n), F32),
        compiler_params=_params("parallel"),
    )(x16, wf16)


def _cumsum_kernel(f_ref, b_ref, init_ref, lf_ref, c_ref, carry_ref, *, apply_log_sigmoid):
    @pl.when(pl.program_id(1) == 0)
    def _():
        carry_ref[...] = init_ref[0]

    x = f_ref[...]
    if apply_log_sigmoid:
        x = _log_sigmoid(x + b_ref[...])
    tt = x.shape[0]
    row = lax.broadcasted_iota(jnp.int32, (tt, tt), 0)
    col = lax.broadcasted_iota(jnp.int32, (tt, tt), 1)
    tri = (col <= row).astype(F32)
    cs = jnp.dot(tri, x, precision=lax.Precision.HIGHEST, preferred_element_type=F32) + carry_ref[...]
    lf_ref[...] = x
    c_ref[...] = cs
    carry_ref[...] = cs[tt - 1:tt, :]


def _cumsum(f, bias, init, batch, tt, apply_log_sigmoid):
    rows, n = f.shape
    nt = rows // batch // tt
    return pl.pallas_call(
        functools.partial(_cumsum_kernel, apply_log_sigmoid=apply_log_sigmoid),
        grid=(batch, nt),
        in_specs=[pl.BlockSpec((tt, n), lambda b, j: (b * nt + j, 0)),
                  pl.BlockSpec((1, n), lambda b, j: (0, 0)),
                  pl.BlockSpec((1, 1, n), lambda b, j: (b, 0, 0))],
        out_specs=[pl.BlockSpec((tt, n), lambda b, j: (b * nt + j, 0)),
                   pl.BlockSpec((tt, n), lambda b, j: (b * nt + j, 0))],
        out_shape=[jax.ShapeDtypeStruct((rows, n), F32), jax.ShapeDtypeStruct((rows, n), F32)],
        scratch_shapes=[pltpu.VMEM((1, n), F32)],
        compiler_params=_params("arbitrary", "arbitrary"),
    )(f, bias, init)


def _conv_kernel(val_ref, glu_ref, gate_ref, buf_ref, w_ref, cb_ref, g_ref, be_ref,
                 out_ref, st_ref, u_ref, y_ref, sh_ref, *, tt, nt):
    j = pl.program_id(1)
    chans = val_ref.shape[1]

    @pl.when(j == 0)
    def _():
        u_ref[0:CONV_HALO, :] = buf_ref[0]

    u_ref[CONV_HALO:CONV_HALO + tt, :] = val_ref[...] * jax.nn.sigmoid(glu_ref[...])
    sh_rows = sh_ref.shape[1]
    for b in range(1, SUBLANES):
        sh_ref[b - 1] = u_ref[b:b + sh_rows, :]
    first = CONV_HALO - (CONV_WIDTH - 1)
    rc = 32
    for r in range(tt // rc):
        acc = jnp.broadcast_to(cb_ref[...], (rc, chans))
        for k in range(CONV_WIDTH):
            a, b = divmod(first + k, SUBLANES)
            lo = r * rc + a * SUBLANES
            src = u_ref[lo:lo + rc, :] if b == 0 else sh_ref[b - 1, lo:lo + rc, :]
            acc = acc + src * w_ref[k:k + 1, :]
        y_ref[r * rc:(r + 1) * rc, :] = acc
    y = y_ref[...]
    mu = jnp.mean(y, axis=-1, keepdims=True)
    d = y - mu
    var = jnp.mean(d * d, axis=-1, keepdims=True)
    yn = d * lax.rsqrt(var + LN_EPS) * g_ref[...] + be_ref[...]
    out_ref[...] = (_silu(yn) * _silu(gate_ref[...])).astype(BF16)

    tail = u_ref[tt:tt + CONV_HALO, :]

    @pl.when(j == nt - 1)
    def _():
        st_ref[0] = tail

    u_ref[0:CONV_HALO, :] = tail


def _conv_branch(z32, buf, conv_w, conv_b, ln_g, ln_b, batch, t_len, width):
    tt = min(t_len, 256)
    nt = t_len // tt
    rows = batch * t_len
    buf32 = jnp.pad(buf.astype(F32), ((0, 0), (CONV_HALO - (CONV_WIDTH - 1), 0), (0, 0)))
    w32 = jnp.pad(conv_w, ((0, CONV_HALO - CONV_WIDTH), (0, 0)))
    row_map = lambda c: (lambda b, j: (b * nt + j, c))
    vec = pl.BlockSpec((1, width), lambda b, j: (0, 0))
    out, st = pl.pallas_call(
        functools.partial(_conv_kernel, tt=tt, nt=nt),
        grid=(batch, nt),
        in_specs=[pl.BlockSpec((tt, width), row_map(0)),
                  pl.BlockSpec((tt, width), row_map(1)),
                  pl.BlockSpec((tt, width), row_map(2)),
                  pl.BlockSpec((1, CONV_HALO, width), lambda b, j: (b, 0, 0)),
                  pl.BlockSpec((CONV_HALO, width), lambda b, j: (0, 0)),
                  vec, vec, vec],
        out_specs=[pl.BlockSpec((tt, width), lambda b, j: (b * nt + j, 0)),
                   pl.BlockSpec((1, CONV_HALO, width), lambda b, j: (b, 0, 0))],
        out_shape=[jax.ShapeDtypeStruct((rows, width), BF16),
                   jax.ShapeDtypeStruct((batch, CONV_HALO, width), F32)],
        scratch_shapes=[pltpu.VMEM((CONV_HALO + tt, width), F32), pltpu.VMEM((tt, width), F32),
                        pltpu.VMEM((SUBLANES - 1, CONV_HALO - SUBLANES + tt, width), F32)],
        compiler_params=_params("arbitrary", "arbitrary"),
    )(z32, z32, z32, buf32, w32, conv_b[None, :], ln_g[None, :], ln_b[None, :])
    return out, st[:, CONV_HALO - (CONV_WIDTH - 1):, :]


def _ret_kernel(lg_ref, q_ref, k_ref, v_ref, g_ref, cos_ref, sin_ref, gg_ref, gb_ref, s0_ref,
                out_ref, sout_ref, s_ref, dec_ref, *, blk, nblk):
    n = pl.program_id(1)
    n_h = s_ref.shape[0]

    @pl.when(n == 0)
    def _():
        s_ref[...] = s0_ref[0]
        i = lax.broadcasted_iota(jnp.int32, (blk, blk), 0)
        j = lax.broadcasted_iota(jnp.int32, (blk, blk), 1)
        diff = i - j
        for h in range(n_h):
            dec_ref[h] = jnp.where(diff >= 0, jnp.exp(jnp.maximum(diff, 0).astype(F32) * lg_ref[h]), 0.0)

    cos = cos_ref[...]
    sin = sin_ref[...]

    def rot(x):
        return x * cos + pltpu.roll(x, HEAD_DIM // 2, 1) * sin

    idx = lax.broadcasted_iota(jnp.int32, (blk, 1), 0).astype(F32)
    for h in range(n_h):
        sl = slice(h * HEAD_DIM, (h + 1) * HEAD_DIM)
        lg = lg_ref[h]
        q = rot(q_ref[:, sl]) * SCALE
        k = rot(k_ref[:, sl])
        v = v_ref[:, sl]
        q_dec = jnp.exp((idx + 1.0) * lg)
        k_dec = jnp.exp((blk - 1.0 - idx) * lg)
        s_dec = jnp.exp(jnp.full((1, HEAD_DIM), blk, F32) * lg)

        state = s_ref[h]
        inner = lax.dot_general(q.astype(BF16), k.astype(BF16), _NT, preferred_element_type=F32) * dec_ref[h]
        o = (jnp.dot(inner.astype(BF16), v, preferred_element_type=F32)
             + jnp.dot((q * q_dec).astype(BF16), state.astype(BF16), preferred_element_type=F32))
        s_ref[h] = s_dec * state + lax.dot_general((k * k_dec).astype(BF16), v, _TN,
                                                   preferred_element_type=F32)

        mu = jnp.mean(o, axis=-1, keepdims=True)
        d = o - mu
        var = jnp.mean(d * d, axis=-1, keepdims=True)
        y = d * lax.rsqrt(var + LN_EPS) * gg_ref[:, sl] + gb_ref[:, sl]
        out_ref[:, sl] = (y * _silu(g_ref[:, sl])).astype(BF16)

    @pl.when(n == nblk - 1)
    def _():
        sout_ref[0] = s_ref[...]


def _retention(z32, z16, cos2, sin2, gn_g, gn_b, s0, log_gamma, batch, t_len, blk, col0):
    n_h = s0.shape[1]
    width = n_h * HEAD_DIM
    nblk = t_len // blk
    rows = batch * t_len
    col = lambda c: (lambda b, n: (b * nblk + n, c))
    blk_spec = lambda c: pl.BlockSpec((blk, width), col(c))
    tab = pl.BlockSpec((blk, HEAD_DIM), lambda b, n: (n, 0))
    vec = pl.BlockSpec((1, width), lambda b, n: (0, 0))
    st = pl.BlockSpec((1, n_h, HEAD_DIM, HEAD_DIM), lambda b, n: (b, 0, 0, 0))
    return pl.pallas_call(
        functools.partial(_ret_kernel, blk=blk, nblk=nblk),
        grid=(batch, nblk),
        in_specs=[pl.BlockSpec(memory_space=pltpu.SMEM),
                  blk_spec(col0), blk_spec(col0 + 1), blk_spec(col0 + 2), blk_spec(col0 + 3),
                  tab, tab, vec, vec, st],
        out_specs=[blk_spec(0), st],
        out_shape=[jax.ShapeDtypeStruct((rows, width), BF16),
                   jax.ShapeDtypeStruct(s0.shape, F32)],
        scratch_shapes=[pltpu.VMEM((n_h, HEAD_DIM, HEAD_DIM), F32), pltpu.VMEM((n_h, blk, blk), F32)],
        compiler_params=_params("arbitrary", "arbitrary"),
    )(log_gamma, z32, z32, z16, z32, cos2, sin2, gn_g[None, :], gn_b[None, :], s0)


def _fox_prompt_kernel(qt_ref, kt_ref, q_ref, k_ref, vt_ref, g_ref, cq_ref, ck_ref, out_ref,
                       m_ref, l_ref, acc_ref, *, n_h):
    p = pl.program_id(1)
    qi = qt_ref[p]
    ki = kt_ref[p]
    tq = q_ref.shape[0]

    @pl.when(ki == 0)
    def _():
        m_ref[...] = jnp.full(m_ref.shape, NEG_INF, F32)
        l_ref[...] = jnp.zeros(l_ref.shape, F32)
        acc_ref[...] = jnp.zeros(acc_ref.shape, F32)

    def accumulate(diagonal):
        ck2 = ck_ref[...] * LOG2E
        cq2 = cq_ref[0] * LOG2E
        if diagonal:
            key = lax.broadcasted_iota(jnp.int32, (tq, tq), 0)
            qry = lax.broadcasted_iota(jnp.int32, (tq, tq), 1)
            visible = key <= qry
        for h in range(n_h):
            sl = slice(h * HEAD_DIM, (h + 1) * HEAD_DIM)
            t = lax.dot_general(k_ref[:, sl], q_ref[:, sl], _NT, preferred_element_type=F32) * (SCALE * LOG2E)
            t = t - ck2[:, h:h + 1]
            if diagonal:
                t = jnp.where(visible, t, NEG_INF)
            cq_h = cq2[h:h + 1, :]
            m_prev = m_ref[h]
            m_new = jnp.maximum(m_prev, cq_h + jnp.max(t, axis=0, keepdims=True))
            alpha = jnp.exp2(m_prev - m_new)
            prob = jnp.exp2(t - (m_new - cq_h))
            l_ref[h] = alpha * l_ref[h] + jnp.sum(prob, axis=0, keepdims=True)
            acc_ref[h] = alpha * acc_ref[h] + jnp.dot(vt_ref[0, sl, :], prob.astype(BF16),
                                                      preferred_element_type=F32)
            m_ref[h] = m_new

    @pl.when(ki < qi)
    def _():
        accumulate(False)

    @pl.when(ki == qi)
    def _():
        accumulate(True)
        for h in range(n_h):
            sl = slice(h * HEAD_DIM, (h + 1) * HEAD_DIM)
            o = (acc_ref[h] / l_ref[h]).T
            out_ref[:, sl] = (o * _silu(g_ref[:, sl])).astype(BF16)


def _fox_prompt(z32, z16, c, c_t, batch, t_len, n_h, col0, tq):
    nq = t_len // tq
    width = n_h * HEAD_DIM
    rows = batch * t_len
    v_t = jnp.swapaxes(z16[:, (col0 + 2) * width:(col0 + 3) * width].reshape(batch, t_len, width), 1, 2)
    pairs = [(a, b) for a in range(nq) for b in range(a + 1)]
    q_tab = jnp.asarray(np.array([a for a, _ in pairs], np.int32))
    k_tab = jnp.asarray(np.array([b for _, b in pairs], np.int32))
    qmap = lambda c_: (lambda b, p, qt, kt: (b * nq + qt[p], c_))
    kmap = lambda c_: (lambda b, p, qt, kt: (b * nq + kt[p], c_))
    grid_spec = pltpu.PrefetchScalarGridSpec(
        num_scalar_prefetch=2,
        grid=(batch, len(pairs)),
        in_specs=[pl.BlockSpec((tq, width), qmap(col0)),
                  pl.BlockSpec((tq, width), kmap(col0 + 1)),
                  pl.BlockSpec((1, width, tq), lambda b, p, qt, kt: (b, 0, kt[p])),
                  pl.BlockSpec((tq, width), qmap(col0 + 3)),
                  pl.BlockSpec((1, n_h, tq), lambda b, p, qt, kt: (b, 0, qt[p])),
                  pl.BlockSpec((tq, LANES), kmap(0))],
        out_specs=pl.BlockSpec((tq, width), qmap(0)),
        scratch_shapes=[pltpu.VMEM((n_h, 1, tq), F32), pltpu.VMEM((n_h, 1, tq), F32),
                        pltpu.VMEM((n_h, HEAD_DIM, tq), F32)],
    )
    return pl.pallas_call(
        functools.partial(_fox_prompt_kernel, n_h=n_h),
        grid_spec=grid_spec,
        out_shape=jax.ShapeDtypeStruct((rows, width), BF16),
        compiler_params=_params("arbitrary", "arbitrary"),
    )(q_tab, k_tab, z16, z16, v_t, z32, c_t, c)


def _fox_sample_kernel(q_ref, kc_ref, vc_ref, kn_ref, vn_ref, g_ref, cq_ref, ckc_ref, ckn_ref, out_ref, *, n_h):
    t_q = q_ref.shape[0]
    row = lax.broadcasted_iota(jnp.int32, (t_q, t_q), 0)
    col = lax.broadcasted_iota(jnp.int32, (t_q, t_q), 1)
    causal = col <= row
    for h in range(n_h):
        sl = slice(h * HEAD_DIM, (h + 1) * HEAD_DIM)
        q = q_ref[:, sl]
        cq = cq_ref[0, h]
        s_c = lax.dot_general(q, kc_ref[:, h, :].astype(BF16), _NT, preferred_element_type=F32) * SCALE
        s_c = s_c + cq - ckc_ref[0, h]
        s_n = lax.dot_general(q, kn_ref[:, sl], _NT, preferred_element_type=F32) * SCALE
        s_n = jnp.where(causal, s_n + cq - ckn_ref[0, h], NEG_INF)
        m = jnp.maximum(jnp.max(s_c, axis=-1, keepdims=True), jnp.max(s_n, axis=-1, keepdims=True))
        p_c = jnp.exp(s_c - m)
        p_n = jnp.exp(s_n - m)
        denom = jnp.sum(p_c, axis=-1, keepdims=True) + jnp.sum(p_n, axis=-1, keepdims=True)
        o = (jnp.dot(p_c.astype(BF16), vc_ref[:, h, :].astype(BF16), preferred_element_type=F32)
             + jnp.dot(p_n.astype(BF16), vn_ref[:, sl], preferred_element_type=F32)) / denom
        out_ref[:, sl] = (o * _silu(g_ref[:, sl])).astype(BF16)


def _fox_sample(z32, z16, k_cache, v_cache, layer, cq, ckc, ckn, batch, t_len, n_h, col0):
    past = k_cache.shape[2]
    rows = batch * t_len
    width = n_h * HEAD_DIM
    col = lambda c: (lambda b: (b, c))
    cache = pl.BlockSpec((None, None, past, n_h, HEAD_DIM), lambda b: (layer, b, 0, 0, 0))
    blk = lambda m: pl.BlockSpec((t_len, width), m)
    return pl.pallas_call(
        functools.partial(_fox_sample_kernel, n_h=n_h),
        grid=(batch,),
        in_specs=[blk(col(col0)), cache, cache, blk(col(col0 + 1)), blk(col(col0 + 2)), blk(col(col0 + 3)),
                  pl.BlockSpec((1, n_h, t_len, 1), lambda b: (b, 0, 0, 0)),
                  pl.BlockSpec((1, n_h, 1, past), lambda b: (b, 0, 0, 0)),
                  pl.BlockSpec((1, n_h, 1, t_len), lambda b: (b, 0, 0, 0))],
        out_specs=blk(col(0)),
        out_shape=jax.ShapeDtypeStruct((rows, width), BF16),
        compiler_params=_params("parallel"),
    )(z16, k_cache, v_cache, z16, z16, z32, cq, ckc, ckn)


def _toeplitz_row(table):
    n_heads = table.shape[0]
    used = BAND_PAST + BAND_QTILE
    far = table[:, 2 * REL_CLIP:]
    row = jnp.concatenate([jnp.broadcast_to(far, (n_heads, BAND_PAST - REL_CLIP)), table[:, ::-1]], axis=1)
    assert row.shape[1] >= used and TOEPLITZ_WIDTH >= used + BAND_QTILE - 1
    tail = jnp.broadcast_to(far, (n_heads, TOEPLITZ_WIDTH - used))
    return jnp.concatenate([row[:, :used], tail], axis=1).astype(F32)


def _toeplitz(row, n_rows):
    return pltpu.roll(jnp.broadcast_to(row, (n_rows, row.shape[1])), 0, 1, stride=1, stride_axis=0)


def _band_prompt_kernel(*refs, n_h):
    nkb = BAND_KBLOCKS
    trev_ref, q_ref = refs[0], refs[1]
    k_refs = refs[2:2 + nkb]
    v_refs = refs[2 + nkb:2 + 2 * nkb]
    g_ref, out_ref, bias_ref = refs[2 + 2 * nkb:]
    qi = pl.program_id(1)
    tq = BAND_QTILE
    n_keys = nkb * tq

    @pl.when(jnp.logical_and(pl.program_id(0) == 0, qi == 0))
    def _():
        qc = lax.broadcasted_iota(jnp.int32, (tq, n_keys), 0) // CHUNK
        kc = lax.broadcasted_iota(jnp.int32, (tq, n_keys), 1) // CHUNK
        visible = jnp.logical_and(kc >= qc, kc <= qc + BAND_CHUNKS)
        for h in range(n_h):
            bias = _toeplitz(trev_ref[h:h + 1, :], tq)[:, :n_keys]
            bias_ref[h] = jnp.where(visible, bias, NEG_INF).T

    for h in range(n_h):
        sl = slice(h * HEAD_DIM, (h + 1) * HEAD_DIM)
        q = q_ref[:, sl]
        scores = []
        for jb in range(nkb):
            s = lax.dot_general(k_refs[jb][:, sl], q, _NT, preferred_element_type=F32) * SCALE
            s = s + bias_ref[h, jb * tq:(jb + 1) * tq, :]
            scores.append(jnp.where(qi + jb >= nkb - 1, s, NEG_INF))
        m = functools.reduce(jnp.maximum, [jnp.max(s, axis=0, keepdims=True) for s in scores])
        probs = [jnp.exp(s - m) for s in scores]
        denom = sum(jnp.sum(p, axis=0, keepdims=True) for p in probs)
        o_t = sum(jnp.dot(v_refs[jb][0, sl, :], p.astype(BF16), preferred_element_type=F32)
                  for jb, p in enumerate(probs)) / denom
        out_ref[:, sl] = (o_t.T * _silu(g_ref[:, sl])).astype(BF16)


def _band_prompt(z32, z16, trev, batch, t_len, n_h, col0):
    tq = BAND_QTILE
    nq = t_len // tq
    nkb = BAND_KBLOCKS
    width = n_h * HEAD_DIM
    rows = batch * t_len
    qmap = lambda c: (lambda b, i: (b * nq + i, c))
    kmap = lambda c, jb: (lambda b, i: (b * nq + jnp.maximum(i + jb - (nkb - 1), 0), c))
    blk = lambda m: pl.BlockSpec((tq, width), m)
    v_t = jnp.swapaxes(z16[:, (col0 + 2) * width:(col0 + 3) * width].reshape(batch, t_len, width), 1, 2)
    vmap = lambda jb: (lambda b, i: (b, 0, jnp.maximum(i + jb - (nkb - 1), 0)))
    in_specs = ([pl.BlockSpec(trev.shape, lambda b, i: (0, 0)), blk(qmap(col0))]
                + [blk(kmap(col0 + 1, jb)) for jb in range(nkb)]
                + [pl.BlockSpec((1, width, tq), vmap(jb)) for jb in range(nkb)]
                + [blk(qmap(col0 + 3))])
    return pl.pallas_call(
        functools.partial(_band_prompt_kernel, n_h=n_h),
        grid=(batch, nq),
        in_specs=in_specs,
        out_specs=blk(qmap(0)),
        out_shape=jax.ShapeDtypeStruct((rows, width), BF16),
        scratch_shapes=[pltpu.VMEM((n_h, nkb * tq, tq), F32)],
        compiler_params=_params("arbitrary", "arbitrary"),
    )(trev, z16, *([z16] * nkb), *([v_t] * nkb), z32)


def _band_sample_kernel(trev_ref, q_ref, kc_ref, vc_ref, kn_ref, vn_ref, g_ref, out_ref, bias_ref,
                        *, n_h, past):
    t_q = q_ref.shape[0]
    n_cache = kc_ref.shape[0]
    width = bias_ref.shape[2]

    @pl.when(pl.program_id(0) == 0)
    def _():
        i = lax.broadcasted_iota(jnp.int32, (t_q, width), 0)
        j = lax.broadcasted_iota(jnp.int32, (t_q, width), 1)
        q_pos = past + i
        k_pos = past - n_cache + j
        qc = q_pos // CHUNK
        kc = k_pos // CHUNK
        visible = jnp.logical_and(jnp.logical_and(k_pos >= 0, j < n_cache + t_q),
                                  jnp.logical_and(kc <= qc, kc >= qc - BAND_CHUNKS))
        for h in range(n_h):
            bias_ref[h] = jnp.where(visible, _toeplitz(trev_ref[h:h + 1, :], t_q), NEG_INF)

    for h in range(n_h):
        sl = slice(h * HEAD_DIM, (h + 1) * HEAD_DIM)
        q = q_ref[:, sl]
        s_c = lax.dot_general(q, kc_ref[:, h, :].astype(BF16), _NT, preferred_element_type=F32) * SCALE
        s_c = s_c + bias_ref[h, :, 0:n_cache]
        s_n = lax.dot_general(q, kn_ref[:, sl], _NT, preferred_element_type=F32) * SCALE
        s_n = s_n + bias_ref[h, :, n_cache:n_cache + t_q]
        m = jnp.maximum(jnp.max(s_c, axis=-1, keepdims=True), jnp.max(s_n, axis=-1, keepdims=True))
        p_c = jnp.exp(s_c - m)
        p_n = jnp.exp(s_n - m)
        denom = jnp.sum(p_c, axis=-1, keepdims=True) + jnp.sum(p_n, axis=-1, keepdims=True)
        o = (jnp.dot(p_c.astype(BF16), vc_ref[:, h, :].astype(BF16), preferred_element_type=F32)
             + jnp.dot(p_n.astype(BF16), vn_ref[:, sl], preferred_element_type=F32)) / denom
        out_ref[:, sl] = (o * _silu(g_ref[:, sl])).astype(BF16)


def _band_sample(z32, z16, k_cache, v_cache, layer, trev, batch, t_len, n_h, col0, past):
    width = n_h * HEAD_DIM
    rows = batch * t_len
    n_cache = k_cache.shape[2]
    assert n_cache == BAND_PAST and past >= BAND_PAST and t_len <= BAND_QTILE
    col = lambda c: (lambda b: (b, c))
    cache = pl.BlockSpec((None, None, n_cache, n_h, HEAD_DIM), lambda b: (layer, b, 0, 0, 0))
    blk = lambda m: pl.BlockSpec((t_len, width), m)
    return pl.pallas_call(
        functools.partial(_band_sample_kernel, n_h=n_h, past=past),
        grid=(batch,),
        in_specs=[pl.BlockSpec(trev.shape, lambda b: (0, 0)), blk(col(col0)), cache, cache,
                  blk(col(col0 + 1)), blk(col(col0 + 2)), blk(col(col0 + 3))],
        out_specs=blk(col(0)),
        out_shape=jax.ShapeDtypeStruct((rows, width), BF16),
        scratch_shapes=[pltpu.VMEM((n_h, t_len, TOEPLITZ_WIDTH), F32)],
        compiler_params=_params("arbitrary"),
    )(trev, z16, k_cache, v_cache, z16, z16, z32)


def _outproj_kernel(m0, m1, m2, m3, w_ref, x_ref, g_ref, b_ref, y_ref, y16_ref, r_ref, mu_ref, rs_ref,
                    *, alpha, n_tiles):
    i = pl.program_id(0)
    j = pl.program_id(1)
    branches = (m0, m1, m2, m3)
    width = m0.shape[1]
    tn = w_ref.shape[1]
    n_col = r_ref.shape[2] // tn
    cur = i % 2
    prev = 1 - cur

    @pl.when(jnp.logical_and(i >= 1, j == 0))
    def _():
        for r0 in range(0, r_ref.shape[1], OUTPROJ_MCHUNK):
            rows = slice(r0, min(r0 + OUTPROJ_MCHUNK, r_ref.shape[1]))
            r = r_ref[prev, rows, :]
            mu = jnp.mean(r, axis=-1, keepdims=True)
            d = r - mu
            mu_ref[rows, :] = mu
            rs_ref[rows, :] = lax.rsqrt(jnp.mean(d * d, axis=-1, keepdims=True) + LN_EPS)

    for jj in range(n_col):
        cols = slice(jj * tn, (jj + 1) * tn)

        @pl.when(jnp.logical_and(i < n_tiles, j == jj))
        def _(cols=cols):
            acc = alpha * x_ref[...]
            for a in range(4):
                acc = acc + jnp.dot(branches[a][...], w_ref[a * width:(a + 1) * width, :],
                                    preferred_element_type=F32)
            r_ref[cur, :, cols] = acc

        @pl.when(jnp.logical_and(i >= 1, j == jj))
        def _(cols=cols):
            y = (r_ref[prev, :, cols] - mu_ref[...]) * rs_ref[...] * g_ref[:, cols] + b_ref[:, cols]
            y_ref[...] = y
            y16_ref[...] = y.astype(BF16)


def _outproj(branches, w16, x32, ln_g, ln_b, alpha, tm):
    m, d_model = x32.shape
    width = branches[0].shape[1]
    tn = OUTPROJ_TN
    n_col = d_model // tn
    n_tiles = m // tm
    in_row = lambda i: jnp.minimum(i, n_tiles - 1)
    out_map = lambda i, j: (jnp.maximum(i - 1, 0), jnp.where(i == 0, 0, j))
    vec = pl.BlockSpec((1, d_model), lambda i, j: (0, 0))
    return pl.pallas_call(
        functools.partial(_outproj_kernel, alpha=alpha, n_tiles=n_tiles),
        grid=(n_tiles + 1, n_col),
        in_specs=[pl.BlockSpec((tm, width), lambda i, j: (in_row(i), 0), pipeline_mode=pl.Buffered(1))
                  for _ in range(4)]
                 + [pl.BlockSpec((4 * width, tn), lambda i, j: (0, j)),
                    pl.BlockSpec((tm, tn), lambda i, j: (in_row(i), j)), vec, vec],
        out_specs=[pl.BlockSpec((tm, tn), out_map), pl.BlockSpec((tm, tn), out_map)],
        out_shape=[jax.ShapeDtypeStruct((m, d_model), F32), jax.ShapeDtypeStruct((m, d_model), BF16)],
        scratch_shapes=[pltpu.VMEM((2, tm, d_model), F32), pltpu.VMEM((tm, 1), F32), pltpu.VMEM((tm, 1), F32)],
        compiler_params=_params("arbitrary", "arbitrary"),
    )(*branches, w16, x32, ln_g[None, :], ln_b[None, :])


def _rope_tables(pos0, t_len):
    half = HEAD_DIM // 2
    inv = ROPE_BASE ** (-jnp.arange(half, dtype=F32) / half)
    ang = (pos0 + jnp.arange(t_len)).astype(F32)[:, None] * inv[None, :]
    cos = jnp.cos(ang)
    sin = jnp.sin(ang)
    return jnp.concatenate([cos, cos], axis=1), jnp.concatenate([-sin, sin], axis=1)


def _head_major(c, batch, t_len, n_h):
    return jnp.swapaxes(c.reshape(batch, t_len, -1)[:, :, :n_h], 1, 2)


def _layer(x32, x16, hist, weights, batch, t_len, pos0, alpha):
    (w_main16, w_f16, conv_w, conv_b, conv_ln_g, conv_ln_b, ret_gn_g, ret_gn_b, fox_bf,
     trev, w_out16, ln_g, ln_b, log_gamma) = weights
    width = conv_w.shape[1]
    n_h = width // HEAD_DIM
    rows = batch * t_len
    prompt = hist is None
    tm = min(rows, 512)

    z32, z16 = _inproj(x16, w_main16, min(rows, 1024), 1024)
    f_logits = _forget_logits(x16, w_f16, tm)
    bias_f = jnp.pad(fox_bf, (0, LANES - n_h))[None, :]
    zero_row = jnp.zeros((batch, 1, LANES), F32)

    if prompt:
        conv_buf = jnp.zeros((batch, CONV_WIDTH - 1, width), F32)
        ret_s0 = jnp.zeros((batch, n_h, HEAD_DIM, HEAD_DIM), F32)
        ret_blk = min(t_len, 256)
    else:
        conv_buf, ret_s0, fox_k_c, fox_v_c, fox_lf_c, band_k_c, band_v_c, layer = hist
        ret_blk = t_len

    mix_a, conv_state = _conv_branch(z32, conv_buf, conv_w, conv_b, conv_ln_g, conv_ln_b, batch, t_len, width)

    cos2, sin2 = _rope_tables(pos0, t_len)
    mix_b, ret_state = _retention(z32, z16, cos2, sin2, ret_gn_g, ret_gn_b, ret_s0.astype(F32), log_gamma,
                                  batch, t_len, ret_blk, 3)

    if prompt:
        logf, c = _cumsum(f_logits, bias_f, zero_row, batch, min(t_len, 256), True)
        mix_c = _fox_prompt(z32, z16, c, _head_major(c, batch, t_len, n_h), batch, t_len, n_h, 7,
                            min(t_len, 1024))
    else:
        past = fox_k_c.shape[2]
        lf_c = jnp.pad(fox_lf_c.astype(F32), ((0, 0), (0, 0), (0, LANES - n_h))).reshape(batch * past, LANES)
        _, c_cache = _cumsum(lf_c, bias_f, zero_row, batch, min(past, 256), False)
        init = c_cache.reshape(batch, past, LANES)[:, past - 1:, :]
        logf, c_new = _cumsum(f_logits, bias_f, init, batch, t_len, True)
        cq = _head_major(c_new, batch, t_len, n_h)[..., None]
        ckc = _head_major(c_cache, batch, past, n_h)[:, :, None, :]
        ckn = _head_major(c_new, batch, t_len, n_h)[:, :, None, :]
        mix_c = _fox_sample(z32, z16, fox_k_c, fox_v_c, layer, cq, ckc, ckn, batch, t_len, n_h, 7)

    if prompt:
        mix_d = _band_prompt(z32, z16, trev, batch, t_len, n_h, 11)
    else:
        mix_d = _band_sample(z32, z16, band_k_c, band_v_c, layer, trev, batch, t_len, n_h, 11, pos0)

    y32, y16 = _outproj((mix_a, mix_b, mix_c, mix_d), w_out16, x32, ln_g, ln_b, alpha, min(rows, 512))

    def heads(c0):
        return z32[:, c0 * width:(c0 + 1) * width].reshape(batch, t_len, n_h, HEAD_DIM)

    fox_k, fox_v = heads(8), heads(9)
    band_k, band_v = heads(12), heads(13)
    if prompt:
        keep = min(BAND_PAST, t_len)
        band_k, band_v = band_k[:, t_len - keep:], band_v[:, t_len - keep:]
    logf = logf.reshape(batch, t_len, LANES)[:, :, :n_h]
    return y32, y16, (conv_state, ret_state, fox_k, fox_v, logf, band_k, band_v)


def kernel(x_prompt, x_sample, cache_conv, state_ret, cache_fox_k, cache_fox_v, cache_fox_logf,
           cache_band_k, cache_band_v, w_in, conv_w, conv_b, conv_ln_g, conv_ln_b, ret_gn_g, ret_gn_b,
           fox_bf, rel_bias, w_out, ln_g, ln_b):
    depth = w_in.shape[0]
    alpha = (2.0 * depth) ** 0.25
    b_p, t_p, d_model = x_prompt.shape
    b_s, t_s, _ = x_sample.shape
    past = cache_fox_k.shape[2]
    width = conv_w.shape[2]
    n_h = width // HEAD_DIM
    f_col = 11 * width
    log_gamma = jnp.asarray(np.log1p(-np.exp2(-5.0 - np.arange(n_h))), F32)

    xp32 = x_prompt.reshape(b_p * t_p, d_model)
    xs32 = x_sample.reshape(b_s * t_s, d_model)
    xp16, xs16 = xp32.astype(BF16), xs32.astype(BF16)
    st_p, st_s = [], []
    for l in range(depth):
        w_main16 = jnp.concatenate([w_in[l, :, :f_col], w_in[l, :, f_col + n_h:]], axis=1).astype(BF16)
        w_f16 = jnp.pad(w_in[l, :, f_col:f_col + n_h], ((0, 0), (0, LANES - n_h))).astype(BF16)
        weights = (w_main16, w_f16, conv_w[l], conv_b[l], conv_ln_g[l], conv_ln_b[l], ret_gn_g[l], ret_gn_b[l],
                   fox_bf[l], _toeplitz_row(rel_bias[l]), w_out[l].astype(BF16), ln_g[l], ln_b[l], log_gamma)
        xp32, xp16, sp = _layer(xp32, xp16, None, weights, b_p, t_p, 0, alpha)
        hist = (cache_conv[l], state_ret[l], cache_fox_k, cache_fox_v, cache_fox_logf[l],
                cache_band_k, cache_band_v, l)
        xs32, xs16, ss = _layer(xs32, xs16, hist, weights, b_s, t_s, past, alpha)
        st_p.append(sp)
        st_s.append(ss)

    def stack(states, i):
        return jnp.stack([s[i] for s in states], axis=0)

    return (xp32.reshape(b_p, t_p, d_model), xs32.reshape(b_s, t_s, d_model),
            *[stack(st_p, i) for i in range(7)], *[stack(st_s, i) for i in range(7)])
```

```python
import functools

import numpy as np
import jax
import jax.numpy as jnp
from jax import lax
from jax.experimental import pallas as pl
from jax.experimental.pallas import tpu as pltpu

HEAD_DIM = 128
CHUNK = 64
CONV_WIDTH = 31
BAND_CHUNKS = 8
BAND_PAST = BAND_CHUNKS * CHUNK
REL_CLIP = 128
ROPE_BASE = 10000.0
LN_EPS = 1e-5
SCALE = HEAD_DIM ** -0.5
LOG2E = 1.4426950408889634

F32 = jnp.float32
BF16 = jnp.bfloat16
NEG_INF = float("-inf")

VMEM_LIMIT_BYTES = 52 * 1024 * 1024
LANES = 128
SUBLANES = 8
CONV_HALO = 32
BAND_QTILE = 2 * CHUNK
BAND_KBLOCKS = (BAND_PAST + BAND_QTILE) // BAND_QTILE
TOEPLITZ_WIDTH = 768
OUTPROJ_TN = 1024
OUTPROJ_MCHUNK = 128

_NT = (((1,), (1,)), ((), ()))
_TN = (((0,), (0,)), ((), ()))


def _params(*sem):
    return pltpu.CompilerParams(dimension_semantics=sem, vmem_limit_bytes=VMEM_LIMIT_BYTES)


def _silu(x):
    return x * jax.nn.sigmoid(x)


def _log_sigmoid(x):
    return jnp.minimum(x, 0.0) - jnp.log1p(jnp.exp(-jnp.abs(x)))


def _inproj_kernel(x_ref, w_ref, o32_ref, o16_ref):
    acc = lax.dot_general(x_ref[...], w_ref[...], _NT, preferred_element_type=F32)
    o32_ref[...] = acc
    o16_ref[...] = acc.astype(BF16)


def _inproj(x16, w16_t, tm, tn):
    m, k = x16.shape
    n = w16_t.shape[0]
    return pl.pallas_call(
        _inproj_kernel,
        grid=(m // tm, n // tn),
        in_specs=[pl.BlockSpec((tm, k), lambda i, j: (i, 0)),
                  pl.BlockSpec((tn, k), lambda i, j: (j, 0))],
        out_specs=[pl.BlockSpec((tm, tn), lambda i, j: (i, j)),
                   pl.BlockSpec((tm, tn), lambda i, j: (i, j))],
        out_shape=[jax.ShapeDtypeStruct((m, n), F32), jax.ShapeDtypeStruct((m, n), BF16)],
        compiler_params=_params("parallel", "arbitrary"),
    )(x16, w16_t)


def _forget_kernel(x_ref, w_ref, o_ref):
    o_ref[...] = jnp.dot(x_ref[...], w_ref[...], preferred_element_type=F32)


def _forget_logits(x16, wf16, tm):
    m, k = x16.shape
    n = wf16.shape[1]
    return pl.pallas_call(
        _forget_kernel,
        grid=(m // tm,),
        in_specs=[pl.BlockSpec((tm, k), lambda i: (i, 0)),
                  pl.BlockSpec((k, n), lambda i: (0, 0))],
        out_specs=pl.BlockSpec((tm, n), lambda i: (i, 0)),
        out_shape=jax.ShapeDtypeStruct((m, n), F32),
        compiler_params=_params("parallel"),
    )(x16, wf16)


def _cumsum_kernel(f_ref, b_ref, init_ref, lf_ref, c_ref, carry_ref, *, apply_log_sigmoid):
    @pl.when(pl.program_id(1) == 0)
    def _():
        carry_ref[...] = init_ref[0]

    x = f_ref[...]
    if apply_log_sigmoid:
        x = _log_sigmoid(x + b_ref[...])
    tt = x.shape[0]
    row = lax.broadcasted_iota(jnp.int32, (tt, tt), 0)
    col = lax.broadcasted_iota(jnp.int32, (tt, tt), 1)
    tri = (col <= row).astype(F32)
    cs = jnp.dot(tri, x, precision=lax.Precision.HIGHEST, preferred_element_type=F32) + carry_ref[...]
    lf_ref[...] = x
    c_ref[...] = cs
    carry_ref[...] = cs[tt - 1:tt, :]


def _cumsum(f, bias, init, batch, tt, apply_log_sigmoid):
    rows, n = f.shape
    nt = rows // batch // tt
    return pl.pallas_call(
        functools.partial(_cumsum_kernel, apply_log_sigmoid=apply_log_sigmoid),
        grid=(batch, nt),
        in_specs=[pl.BlockSpec((tt, n), lambda b, j: (b * nt + j, 0)),
                  pl.BlockSpec((1, n), lambda b, j: (0, 0)),
                  pl.BlockSpec((1, 1, n), lambda b, j: (b, 0, 0))],
        out_specs=[pl.BlockSpec((tt, n), lambda b, j: (b * nt + j, 0)),
                   pl.BlockSpec((tt, n), lambda b, j: (b * nt + j, 0))],
        out_shape=[jax.ShapeDtypeStruct((rows, n), F32), jax.ShapeDtypeStruct((rows, n), F32)],
        scratch_shapes=[pltpu.VMEM((1, n), F32)],
        compiler_params=_params("arbitrary", "arbitrary"),
    )(f, bias, init)


def _conv_kernel(val_ref, glu_ref, gate_ref, buf_ref, w_ref, cb_ref, g_ref, be_ref,
                 out_ref, st_ref, u_ref, y_ref, sh_ref, *, tt, nt):
    j = pl.program_id(1)
    chans = val_ref.shape[1]

    @pl.when(j == 0)
    def _():
        u_ref[0:CONV_HALO, :] = buf_ref[0]

    u_ref[CONV_HALO:CONV_HALO + tt, :] = val_ref[...] * jax.nn.sigmoid(glu_ref[...])
    sh_rows = sh_ref.shape[1]
    for b in range(1, SUBLANES):
        sh_ref[b - 1] = u_ref[b:b + sh_rows, :]
    first = CONV_HALO - (CONV_WIDTH - 1)
    rc = 32
    for r in range(tt // rc):
        acc = jnp.broadcast_to(cb_ref[...], (rc, chans))
        for k in range(CONV_WIDTH):
            a, b = divmod(first + k, SUBLANES)
            lo = r * rc + a * SUBLANES
            src = u_ref[lo:lo + rc, :] if b == 0 else sh_ref[b - 1, lo:lo + rc, :]
            acc = acc + src * w_ref[k:k + 1, :]
        y_ref[r * rc:(r + 1) * rc, :] = acc
    y = y_ref[...]
    mu = jnp.mean(y, axis=-1, keepdims=True)
    d = y - mu
    var = jnp.mean(d * d, axis=-1, keepdims=True)
    yn = d * lax.rsqrt(var + LN_EPS) * g_ref[...] + be_ref[...]
    out_ref[...] = (_silu(yn) * _silu(gate_ref[...])).astype(BF16)

    tail = u_ref[tt:tt + CONV_HALO, :]

    @pl.when(j == nt - 1)
    def _():
        st_ref[0] = tail

    u_ref[0:CONV_HALO, :] = tail


def _conv_branch(z32, buf, conv_w, conv_b, ln_g, ln_b, batch, t_len, width):
    tt = min(t_len, 256)
    nt = t_len // tt
    rows = batch * t_len
    buf32 = jnp.pad(buf.astype(F32), ((0, 0), (CONV_HALO - (CONV_WIDTH - 1), 0), (0, 0)))
    w32 = jnp.pad(conv_w, ((0, CONV_HALO - CONV_WIDTH), (0, 0)))
    row_map = lambda c: (lambda b, j: (b * nt + j, c))
    vec = pl.BlockSpec((1, width), lambda b, j: (0, 0))
    out, st = pl.pallas_call(
        functools.partial(_conv_kernel, tt=tt, nt=nt),
        grid=(batch, nt),
        in_specs=[pl.BlockSpec((tt, width), row_map(0)),
                  pl.BlockSpec((tt, width), row_map(1)),
                  pl.BlockSpec((tt, width), row_map(2)),
                  pl.BlockSpec((1, CONV_HALO, width), lambda b, j: (b, 0, 0)),
                  pl.BlockSpec((CONV_HALO, width), lambda b, j: (0, 0)),
                  vec, vec, vec],
        out_specs=[pl.BlockSpec((tt, width), lambda b, j: (b * nt + j, 0)),
                   pl.BlockSpec((1, CONV_HALO, width), lambda b, j: (b, 0, 0))],
        out_shape=[jax.ShapeDtypeStruct((rows, width), BF16),
                   jax.ShapeDtypeStruct((batch, CONV_HALO, width), F32)],
        scratch_shapes=[pltpu.VMEM((CONV_HALO + tt, width), F32), pltpu.VMEM((tt, width), F32),
                        pltpu.VMEM((SUBLANES - 1, CONV_HALO - SUBLANES + tt, width), F32)],
        compiler_params=_params("arbitrary", "arbitrary"),
    )(z32, z32, z32, buf32, w32, conv_b[None, :], ln_g[None, :], ln_b[None, :])
    return out, st[:, CONV_HALO - (CONV_WIDTH - 1):, :]


def _ret_kernel(lg_ref, q_ref, k_ref, v_ref, g_ref, cos_ref, sin_ref, gg_ref, gb_ref, s0_ref,
                out_ref, sout_ref, s_ref, dec_ref, *, blk, nblk):
    n = pl.program_id(1)
    n_h = s_ref.shape[0]

    @pl.when(n == 0)
    def _():
        s_ref[...] = s0_ref[0]
        i = lax.broadcasted_iota(jnp.int32, (blk, blk), 0)
        j = lax.broadcasted_iota(jnp.int32, (blk, blk), 1)
        diff = i - j
        for h in range(n_h):
            dec_ref[h] = jnp.where(diff >= 0, jnp.exp(jnp.maximum(diff, 0).astype(F32) * lg_ref[h]), 0.0)

    cos = cos_ref[...]
    sin = sin_ref[...]

    def rot(x):
        return x * cos + pltpu.roll(x, HEAD_DIM // 2, 1) * sin

    idx = lax.broadcasted_iota(jnp.int32, (blk, 1), 0).astype(F32)
    for h in range(n_h):
        sl = slice(h * HEAD_DIM, (h + 1) * HEAD_DIM)
        lg = lg_ref[h]
        q = rot(q_ref[:, sl]) * SCALE
        k = rot(k_ref[:, sl])
        v = v_ref[:, sl]
        q_dec = jnp.exp((idx + 1.0) * lg)
        k_dec = jnp.exp((blk - 1.0 - idx) * lg)
        s_dec = jnp.exp(jnp.full((1, HEAD_DIM), blk, F32) * lg)

        state = s_ref[h]
        inner = lax.dot_general(q.astype(BF16), k.astype(BF16), _NT, preferred_element_type=F32) * dec_ref[h]
        o = (jnp.dot(inner.astype(BF16), v, preferred_element_type=F32)
             + jnp.dot((q * q_dec).astype(BF16), state.astype(BF16), preferred_element_type=F32))
        s_ref[h] = s_dec * state + lax.dot_general((k * k_dec).astype(BF16), v, _TN,
                                                   preferred_element_type=F32)

        mu = jnp.mean(o, axis=-1, keepdims=True)
        d = o - mu
        var = jnp.mean(d * d, axis=-1, keepdims=True)
        y = d * lax.rsqrt(var + LN_EPS) * gg_ref[:, sl] + gb_ref[:, sl]
        out_ref[:, sl] = (y * _silu(g_ref[:, sl])).astype(BF16)

    @pl.when(n == nblk - 1)
    def _():
        sout_ref[0] = s_ref[...]


def _retention(z32, z16, cos2, sin2, gn_g, gn_b, s0, log_gamma, batch, t_len, blk, col0):
    n_h = s0.shape[1]
    width = n_h * HEAD_DIM
    nblk = t_len // blk
    rows = batch * t_len
    col = lambda c: (lambda b, n: (b * nblk + n, c))
    blk_spec = lambda c: pl.BlockSpec((blk, width), col(c))
    tab = pl.BlockSpec((blk, HEAD_DIM), lambda b, n: (n, 0))
    vec = pl.BlockSpec((1, width), lambda b, n: (0, 0))
    st = pl.BlockSpec((1, n_h, HEAD_DIM, HEAD_DIM), lambda b, n: (b, 0, 0, 0))
    return pl.pallas_call(
        functools.partial(_ret_kernel, blk=blk, nblk=nblk),
        grid=(batch, nblk),
        in_specs=[pl.BlockSpec(memory_space=pltpu.SMEM),
                  blk_spec(col0), blk_spec(col0 + 1), blk_spec(col0 + 2), blk_spec(col0 + 3),
                  tab, tab, vec, vec, st],
        out_specs=[blk_spec(0), st],
        out_shape=[jax.ShapeDtypeStruct((rows, width), BF16),
                   jax.ShapeDtypeStruct(s0.shape, F32)],
        scratch_shapes=[pltpu.VMEM((n_h, HEAD_DIM, HEAD_DIM), F32), pltpu.VMEM((n_h, blk, blk), F32)],
        compiler_params=_params("arbitrary", "arbitrary"),
    )(log_gamma, z32, z32, z16, z32, cos2, sin2, gn_g[None, :], gn_b[None, :], s0)


def _fox_prompt_kernel(qt_ref, kt_ref, q_ref, k_ref, vt_ref, g_ref, cq_ref, ck_ref, out_ref,
                       m_ref, l_ref, acc_ref, *, n_h):
    p = pl.program_id(1)
    qi = qt_ref[p]
    ki = kt_ref[p]
    tq = q_ref.shape[0]

    @pl.when(ki == 0)
    def _():
        m_ref[...] = jnp.full(m_ref.shape, NEG_INF, F32)
        l_ref[...] = jnp.zeros(l_ref.shape, F32)
        acc_ref[...] = jnp.zeros(acc_ref.shape, F32)

    def accumulate(diagonal):
        ck2 = ck_ref[...] * LOG2E
        cq2 = cq_ref[0] * LOG2E
        if diagonal:
            key = lax.broadcasted_iota(jnp.int32, (tq, tq), 0)
            qry = lax.broadcasted_iota(jnp.int32, (tq, tq), 1)
            visible = key <= qry
        for h in range(n_h):
            sl = slice(h * HEAD_DIM, (h + 1) * HEAD_DIM)
            t = lax.dot_general(k_ref[:, sl], q_ref[:, sl], _NT, preferred_element_type=F32) * (SCALE * LOG2E)
            t = t - ck2[:, h:h + 1]
            if diagonal:
                t = jnp.where(visible, t, NEG_INF)
            cq_h = cq2[h:h + 1, :]
            m_prev = m_ref[h]
            m_new = jnp.maximum(m_prev, cq_h + jnp.max(t, axis=0, keepdims=True))
            alpha = jnp.exp2(m_prev - m_new)
            prob = jnp.exp2(t - (m_new - cq_h))
            l_ref[h] = alpha * l_ref[h] + jnp.sum(prob, axis=0, keepdims=True)
            acc_ref[h] = alpha * acc_ref[h] + jnp.dot(vt_ref[0, sl, :], prob.astype(BF16),
                                                      preferred_element_type=F32)
            m_ref[h] = m_new

    @pl.when(ki < qi)
    def _():
        accumulate(False)

    @pl.when(ki == qi)
    def _():
        accumulate(True)
        for h in range(n_h):
            sl = slice(h * HEAD_DIM, (h + 1) * HEAD_DIM)
            o = (acc_ref[h] / l_ref[h]).T
            out_ref[:, sl] = (o * _silu(g_ref[:, sl])).astype(BF16)


def _fox_prompt(z32, z16, c, c_t, batch, t_len, n_h, col0, tq):
    nq = t_len // tq
    width = n_h * HEAD_DIM
    rows = batch * t_len
    v_t = jnp.swapaxes(z16[:, (col0 + 2) * width:(col0 + 3) * width].reshape(batch, t_len, width), 1, 2)
    pairs = [(a, b) for a in range(nq) for b in range(a + 1)]
    q_tab = jnp.asarray(np.array([a for a, _ in pairs], np.int32))
    k_tab = jnp.asarray(np.array([b for _, b in pairs], np.int32))
    qmap = lambda c_: (lambda b, p, qt, kt: (b * nq + qt[p], c_))
    kmap = lambda c_: (lambda b, p, qt, kt: (b * nq + kt[p], c_))
    grid_spec = pltpu.PrefetchScalarGridSpec(
        num_scalar_prefetch=2,
        grid=(batch, len(pairs)),
        in_specs=[pl.BlockSpec((tq, width), qmap(col0)),
                  pl.BlockSpec((tq, width), kmap(col0 + 1)),
                  pl.BlockSpec((1, width, tq), lambda b, p, qt, kt: (b, 0, kt[p])),
                  pl.BlockSpec((tq, width), qmap(col0 + 3)),
                  pl.BlockSpec((1, n_h, tq), lambda b, p, qt, kt: (b, 0, qt[p])),
                  pl.BlockSpec((tq, LANES), kmap(0))],
        out_specs=pl.BlockSpec((tq, width), qmap(0)),
        scratch_shapes=[pltpu.VMEM((n_h, 1, tq), F32), pltpu.VMEM((n_h, 1, tq), F32),
                        pltpu.VMEM((n_h, HEAD_DIM, tq), F32)],
    )
    return pl.pallas_call(
        functools.partial(_fox_prompt_kernel, n_h=n_h),
        grid_spec=grid_spec,
        out_shape=jax.ShapeDtypeStruct((rows, width), BF16),
        compiler_params=_params("arbitrary", "arbitrary"),
    )(q_tab, k_tab, z16, z16, v_t, z32, c_t, c)


def _fox_sample_kernel(q_ref, kc_ref, vc_ref, kn_ref, vn_ref, g_ref, cq_ref, ckc_ref, ckn_ref, out_ref, *, n_h):
    t_q = q_ref.shape[0]
    row = lax.broadcasted_iota(jnp.int32, (t_q, t_q), 0)
    col = lax.broadcasted_iota(jnp.int32, (t_q, t_q), 1)
    causal = col <= row
    for h in range(n_h):
        sl = slice(h * HEAD_DIM, (h + 1) * HEAD_DIM)
        q = q_ref[:, sl]
        cq = cq_ref[0, h]
        s_c = lax.dot_general(q, kc_ref[:, h, :].astype(BF16), _NT, preferred_element_type=F32) * SCALE
        s_c = s_c + cq - ckc_ref[0, h]
        s_n = lax.dot_general(q, kn_ref[:, sl], _NT, preferred_element_type=F32) * SCALE
        s_n = jnp.where(causal, s_n + cq - ckn_ref[0, h], NEG_INF)
        m = jnp.maximum(jnp.max(s_c, axis=-1, keepdims=True), jnp.max(s_n, axis=-1, keepdims=True))
        p_c = jnp.exp(s_c - m)
        p_n = jnp.exp(s_n - m)
        denom = jnp.sum(p_c, axis=-1, keepdims=True) + jnp.sum(p_n, axis=-1, keepdims=True)
        o = (jnp.dot(p_c.astype(BF16), vc_ref[:, h, :].astype(BF16), preferred_element_type=F32)
             + jnp.dot(p_n.astype(BF16), vn_ref[:, sl], preferred_element_type=F32)) / denom
        out_ref[:, sl] = (o * _silu(g_ref[:, sl])).astype(BF16)


def _fox_sample(z32, z16, k_cache, v_cache, layer, cq, ckc, ckn, batch, t_len, n_h, col0):
    past = k_cache.shape[2]
    rows = batch * t_len
    width = n_h * HEAD_DIM
    col = lambda c: (lambda b: (b, c))
    cache = pl.BlockSpec((None, None, past, n_h, HEAD_DIM), lambda b: (layer, b, 0, 0, 0))
    blk = lambda m: pl.BlockSpec((t_len, width), m)
    return pl.pallas_call(
        functools.partial(_fox_sample_kernel, n_h=n_h),
        grid=(batch,),
        in_specs=[blk(col(col0)), cache, cache, blk(col(col0 + 1)), blk(col(col0 + 2)), blk(col(col0 + 3)),
                  pl.BlockSpec((1, n_h, t_len, 1), lambda b: (b, 0, 0, 0)),
                  pl.BlockSpec((1, n_h, 1, past), lambda b: (b, 0, 0, 0)),
                  pl.BlockSpec((1, n_h, 1, t_len), lambda b: (b, 0, 0, 0))],
        out_specs=blk(col(0)),
        out_shape=jax.ShapeDtypeStruct((rows, width), BF16),
        compiler_params=_params("parallel"),
    )(z16, k_cache, v_cache, z16, z16, z32, cq, ckc, ckn)


def _toeplitz_row(table):
    n_heads = table.shape[0]
    used = BAND_PAST + BAND_QTILE
    far = table[:, 2 * REL_CLIP:]
    row = jnp.concatenate([jnp.broadcast_to(far, (n_heads, BAND_PAST - REL_CLIP)), table[:, ::-1]], axis=1)
    assert row.shape[1] >= used and TOEPLITZ_WIDTH >= used + BAND_QTILE - 1
    tail = jnp.broadcast_to(far, (n_heads, TOEPLITZ_WIDTH - used))
    return jnp.concatenate([row[:, :used], tail], axis=1).astype(F32)


def _toeplitz(row, n_rows):
    return pltpu.roll(jnp.broadcast_to(row, (n_rows, row.shape[1])), 0, 1, stride=1, stride_axis=0)


def _band_prompt_kernel(*refs, n_h):
    nkb = BAND_KBLOCKS
    trev_ref, q_ref = refs[0], refs[1]
    k_refs = refs[2:2 + nkb]
    v_refs = refs[2 + nkb:2 + 2 * nkb]
    g_ref, out_ref, bias_ref = refs[2 + 2 * nkb:]
    qi = pl.program_id(1)
    tq = BAND_QTILE
    n_keys = nkb * tq

    @pl.when(jnp.logical_and(pl.program_id(0) == 0, qi == 0))
    def _():
        qc = lax.broadcasted_iota(jnp.int32, (tq, n_keys), 0) // CHUNK
        kc = lax.broadcasted_iota(jnp.int32, (tq, n_keys), 1) // CHUNK
        visible = jnp.logical_and(kc >= qc, kc <= qc + BAND_CHUNKS)
        for h in range(n_h):
            bias = _toeplitz(trev_ref[h:h + 1, :], tq)[:, :n_keys]
            bias_ref[h] = jnp.where(visible, bias, NEG_INF).T

    for h in range(n_h):
        sl = slice(h * HEAD_DIM, (h + 1) * HEAD_DIM)
        q = q_ref[:, sl]
        scores = []
        for jb in range(nkb):
            s = lax.dot_general(k_refs[jb][:, sl], q, _NT, preferred_element_type=F32) * SCALE
            s = s + bias_ref[h, jb * tq:(jb + 1) * tq, :]
            scores.append(jnp.where(qi + jb >= nkb - 1, s, NEG_INF))
        m = functools.reduce(jnp.maximum, [jnp.max(s, axis=0, keepdims=True) for s in scores])
        probs = [jnp.exp(s - m) for s in scores]
        denom = sum(jnp.sum(p, axis=0, keepdims=True) for p in probs)
        o_t = sum(jnp.dot(v_refs[jb][0, sl, :], p.astype(BF16), preferred_element_type=F32)
                  for jb, p in enumerate(probs)) / denom
        out_ref[:, sl] = (o_t.T * _silu(g_ref[:, sl])).astype(BF16)


def _band_prompt(z32, z16, trev, batch, t_len, n_h, col0):
    tq = BAND_QTILE
    nq = t_len // tq
    nkb = BAND_KBLOCKS
    width = n_h * HEAD_DIM
    rows = batch * t_len
    qmap = lambda c: (lambda b, i: (b * nq + i, c))
    kmap = lambda c, jb: (lambda b, i: (b * nq + jnp.maximum(i + jb - (nkb - 1), 0), c))
    blk = lambda m: pl.BlockSpec((tq, width), m)
    v_t = jnp.swapaxes(z16[:, (col0 + 2) * width:(col0 + 3) * width].reshape(batch, t_len, width), 1, 2)
    vmap = lambda jb: (lambda b, i: (b, 0, jnp.maximum(i + jb - (nkb - 1), 0)))
    in_specs = ([pl.BlockSpec(trev.shape, lambda b, i: (0, 0)), blk(qmap(col0))]
                + [blk(kmap(col0 + 1, jb)) for jb in range(nkb)]
                + [pl.BlockSpec((1, width, tq), vmap(jb)) for jb in range(nkb)]
                + [blk(qmap(col0 + 3))])
    return pl.pallas_call(
        functools.partial(_band_prompt_kernel, n_h=n_h),
        grid=(batch, nq),
        in_specs=in_specs,
        out_specs=blk(qmap(0)),
        out_shape=jax.ShapeDtypeStruct((rows, width), BF16),
        scratch_shapes=[pltpu.VMEM((n_h, nkb * tq, tq), F32)],
        compiler_params=_params("arbitrary", "arbitrary"),
    )(trev, z16, *([z16] * nkb), *([v_t] * nkb), z32)


def _band_sample_kernel(trev_ref, q_ref, kc_ref, vc_ref, kn_ref, vn_ref, g_ref, out_ref, bias_ref,
                        *, n_h, past):
    t_q = q_ref.shape[0]
    n_cache = kc_ref.shape[0]
    width = bias_ref.shape[2]

    @pl.when(pl.program_id(0) == 0)
    def _():
        i = lax.broadcasted_iota(jnp.int32, (t_q, width), 0)
        j = lax.broadcasted_iota(jnp.int32, (t_q, width), 1)
        q_pos = past + i
        k_pos = past - n_cache + j
        qc = q_pos // CHUNK
        kc = k_pos // CHUNK
        visible = jnp.logical_and(jnp.logical_and(k_pos >= 0, j < n_cache + t_q),
                                  jnp.logical_and(kc <= qc, kc >= qc - BAND_CHUNKS))
        for h in range(n_h):
            bias_ref[h] = jnp.where(visible, _toeplitz(trev_ref[h:h + 1, :], t_q), NEG_INF)

    for h in range(n_h):
        sl = slice(h * HEAD_DIM, (h + 1) * HEAD_DIM)
        q = q_ref[:, sl]
        s_c = lax.dot_general(q, kc_ref[:, h, :].astype(BF16), _NT, preferred_element_type=F32) * SCALE
        s_c = s_c + bias_ref[h, :, 0:n_cache]
        s_n = lax.dot_general(q, kn_ref[:, sl], _NT, preferred_element_type=F32) * SCALE
        s_n = s_n + bias_ref[h, :, n_cache:n_cache + t_q]
        m = jnp.maximum(jnp.max(s_c, axis=-1, keepdims=True), jnp.max(s_n, axis=-1, keepdims=True))
        p_c = jnp.exp(s_c - m)
        p_n = jnp.exp(s_n - m)
        denom = jnp.sum(p_c, axis=-1, keepdims=True) + jnp.sum(p_n, axis=-1, keepdims=True)
        o = (jnp.dot(p_c.astype(BF16), vc_ref[:, h, :].astype(BF16), preferred_element_type=F32)
             + jnp.dot(p_n.astype(BF16), vn_ref[:, sl], preferred_element_type=F32)) / denom
        out_ref[:, sl] = (o * _silu(g_ref[:, sl])).astype(BF16)


def _band_sample(z32, z16, k_cache, v_cache, layer, trev, batch, t_len, n_h, col0, past):
    width = n_h * HEAD_DIM
    rows = batch * t_len
    n_cache = k_cache.shape[2]
    assert n_cache == BAND_PAST and past >= BAND_PAST and t_len <= BAND_QTILE
    col = lambda c: (lambda b: (b, c))
    cache = pl.BlockSpec((None, None, n_cache, n_h, HEAD_DIM), lambda b: (layer, b, 0, 0, 0))
    blk = lambda m: pl.BlockSpec((t_len, width), m)
    return pl.pallas_call(
        functools.partial(_band_sample_kernel, n_h=n_h, past=past),
        grid=(batch,),
        in_specs=[pl.BlockSpec(trev.shape, lambda b: (0, 0)), blk(col(col0)), cache, cache,
                  blk(col(col0 + 1)), blk(col(col0 + 2)), blk(col(col0 + 3))],
        out_specs=blk(col(0)),
        out_shape=jax.ShapeDtypeStruct((rows, width), BF16),
        scratch_shapes=[pltpu.VMEM((n_h, t_len, TOEPLITZ_WIDTH), F32)],
        compiler_params=_params("arbitrary"),
    )(trev, z16, k_cache, v_cache, z16, z16, z32)


def _outproj_kernel(m0, m1, m2, m3, w_ref, x_ref, g_ref, b_ref, y_ref, y16_ref, r_ref, mu_ref, rs_ref,
                    *, alpha, n_tiles):
    i = pl.program_id(0)
    j = pl.program_id(1)
    branches = (m0, m1, m2, m3)
    width = m0.shape[1]
    tn = w_ref.shape[1]
    n_col = r_ref.shape[2] // tn
    cur = i % 2
    prev = 1 - cur

    @pl.when(jnp.logical_and(i >= 1, j == 0))
    def _():
        for r0 in range(0, r_ref.shape[1], OUTPROJ_MCHUNK):
            rows = slice(r0, min(r0 + OUTPROJ_MCHUNK, r_ref.shape[1]))
            r = r_ref[prev, rows, :]
            mu = jnp.mean(r, axis=-1, keepdims=True)
            d = r - mu
            mu_ref[rows, :] = mu
            rs_ref[rows, :] = lax.rsqrt(jnp.mean(d * d, axis=-1, keepdims=True) + LN_EPS)

    for jj in range(n_col):
        cols = slice(jj * tn, (jj + 1) * tn)

        @pl.when(jnp.logical_and(i < n_tiles, j == jj))
        def _(cols=cols):
            acc = alpha * x_ref[...]
            for a in range(4):
                acc = acc + jnp.dot(branches[a][...], w_ref[a * width:(a + 1) * width, :],
                                    preferred_element_type=F32)
            r_ref[cur, :, cols] = acc

        @pl.when(jnp.logical_and(i >= 1, j == jj))
        def _(cols=cols):
            y = (r_ref[prev, :, cols] - mu_ref[...]) * rs_ref[...] * g_ref[:, cols] + b_ref[:, cols]
            y_ref[...] = y
            y16_ref[...] = y.astype(BF16)


def _outproj(branches, w16, x32, ln_g, ln_b, alpha, tm):
    m, d_model = x32.shape
    width = branches[0].shape[1]
    tn = OUTPROJ_TN
    n_col = d_model // tn
    n_tiles = m // tm
    in_row = lambda i: jnp.minimum(i, n_tiles - 1)
    out_map = lambda i, j: (jnp.maximum(i - 1, 0), jnp.where(i == 0, 0, j))
    vec = pl.BlockSpec((1, d_model), lambda i, j: (0, 0))
    return pl.pallas_call(
        functools.partial(_outproj_kernel, alpha=alpha, n_tiles=n_tiles),
        grid=(n_tiles + 1, n_col),
        in_specs=[pl.BlockSpec((tm, width), lambda i, j: (in_row(i), 0), pipeline_mode=pl.Buffered(1))
                  for _ in range(4)]
                 + [pl.BlockSpec((4 * width, tn), lambda i, j: (0, j)),
                    pl.BlockSpec((tm, tn), lambda i, j: (in_row(i), j)), vec, vec],
        out_specs=[pl.BlockSpec((tm, tn), out_map), pl.BlockSpec((tm, tn), out_map)],
        out_shape=[jax.ShapeDtypeStruct((m, d_model), F32), jax.ShapeDtypeStruct((m, d_model), BF16)],
        scratch_shapes=[pltpu.VMEM((2, tm, d_model), F32), pltpu.VMEM((tm, 1), F32), pltpu.VMEM((tm, 1), F32)],
        compiler_params=_params("arbitrary", "arbitrary"),
    )(*branches, w16, x32, ln_g[None, :], ln_b[None, :])


def _rope_tables(pos0, t_len):
    half = HEAD_DIM // 2
    inv = ROPE_BASE ** (-jnp.arange(half, dtype=F32) / half)
    ang = (pos0 + jnp.arange(t_len)).astype(F32)[:, None] * inv[None, :]
    cos = jnp.cos(ang)
    sin = jnp.sin(ang)
    return jnp.concatenate([cos, cos], axis=1), jnp.concatenate([-sin, sin], axis=1)


def _head_major(c, batch, t_len, n_h):
    return jnp.swapaxes(c.reshape(batch, t_len, -1)[:, :, :n_h], 1, 2)


def _layer(x32, x16, hist, weights, batch, t_len, pos0, alpha):
    (w_main16, w_f16, conv_w, conv_b, conv_ln_g, conv_ln_b, ret_gn_g, ret_gn_b, fox_bf,
     trev, w_out16, ln_g, ln_b, log_gamma) = weights
    width = conv_w.shape[1]
    n_h = width // HEAD_DIM
    rows = batch * t_len
    prompt = hist is None
    tm = min(rows, 512)

    z32, z16 = _inproj(x16, w_main16, min(rows, 1024), 1024)
    f_logits = _forget_logits(x16, w_f16, tm)
    bias_f = jnp.pad(fox_bf, (0, LANES - n_h))[None, :]
    zero_row = jnp.zeros((batch, 1, LANES), F32)

    if prompt:
        conv_buf = jnp.zeros((batch, CONV_WIDTH - 1, width), F32)
        ret_s0 = jnp.zeros((batch, n_h, HEAD_DIM, HEAD_DIM), F32)
        ret_blk = min(t_len, 256)
    else:
        conv_buf, ret_s0, fox_k_c, fox_v_c, fox_lf_c, band_k_c, band_v_c, layer = hist
        ret_blk = t_len

    mix_a, conv_state = _conv_branch(z32, conv_buf, conv_w, conv_b, conv_ln_g, conv_ln_b, batch, t_len, width)

    cos2, sin2 = _rope_tables(pos0, t_len)
    mix_b, ret_state = _retention(z32, z16, cos2, sin2, ret_gn_g, ret_gn_b, ret_s0.astype(F32), log_gamma,
                                  batch, t_len, ret_blk, 3)

    if prompt:
        logf, c = _cumsum(f_logits, bias_f, zero_row, batch, min(t_len, 256), True)
        mix_c = _fox_prompt(z32, z16, c, _head_major(c, batch, t_len, n_h), batch, t_len, n_h, 7,
                            min(t_len, 1024))
    else:
        past = fox_k_c.shape[2]
        lf_c = jnp.pad(fox_lf_c.astype(F32), ((0, 0), (0, 0), (0, LANES - n_h))).reshape(batch * past, LANES)
        _, c_cache = _cumsum(lf_c, bias_f, zero_row, batch, min(past, 256), False)
        init = c_cache.reshape(batch, past, LANES)[:, past - 1:, :]
        logf, c_new = _cumsum(f_logits, bias_f, init, batch, t_len, True)
        cq = _head_major(c_new, batch, t_len, n_h)[..., None]
        ckc = _head_major(c_cache, batch, past, n_h)[:, :, None, :]
        ckn = _head_major(c_new, batch, t_len, n_h)[:, :, None, :]
        mix_c = _fox_sample(z32, z16, fox_k_c, fox_v_c, layer, cq, ckc, ckn, batch, t_len, n_h, 7)

    if prompt:
        mix_d = _band_prompt(z32, z16, trev, batch, t_len, n_h, 11)
    else:
        mix_d = _band_sample(z32, z16, band_k_c, band_v_c, layer, trev, batch, t_len, n_h, 11, pos0)

    y32, y16 = _outproj((mix_a, mix_b, mix_c, mix_d), w_out16, x32, ln_g, ln_b, alpha, min(rows, 512))

    def heads(c0):
        return z32[:, c0 * width:(c0 + 1) * width].reshape(batch, t_len, n_h, HEAD_DIM)

    fox_k, fox_v = heads(8), heads(9)
    band_k, band_v = heads(12), heads(13)
    if prompt:
        keep = min(BAND_PAST, t_len)
        band_k, band_v = band_k[:, t_len - keep:], band_v[:, t_len - keep:]
    logf = logf.reshape(batch, t_len, LANES)[:, :, :n_h]
    return y32, y16, (conv_state, ret_state, fox_k, fox_v, logf, band_k, band_v)


def kernel(x_prompt, x_sample, cache_conv, state_ret, cache_fox_k, cache_fox_v, cache_fox_logf,
           cache_band_k, cache_band_v, w_in, conv_w, conv_b, conv_ln_g, conv_ln_b, ret_gn_g, ret_gn_b,
           fox_bf, rel_bias, w_out, ln_g, ln_b):
    depth = w_in.shape[0]
    alpha = (2.0 * depth) ** 0.25
    b_p, t_p, d_model = x_prompt.shape
    b_s, t_s, _ = x_sample.shape
    past = cache_fox_k.shape[2]
    width = conv_w.shape[2]
    n_h = width // HEAD_DIM
    f_col = 11 * width
    log_gamma = jnp.asarray(np.log1p(-np.exp2(-5.0 - np.arange(n_h))), F32)

    xp32 = x_prompt.reshape(b_p * t_p, d_model)
    xs32 = x_sample.reshape(b_s * t_s, d_model)
    xp16, xs16 = xp32.astype(BF16), xs32.astype(BF16)
    st_p, st_s = [], []
    for l in range(depth):
        w_t = jnp.swapaxes(w_in[l], 0, 1)
        w_main16 = jnp.concatenate([w_t[:f_col], w_t[f_col + n_h:]], axis=0).astype(BF16)
        w_f16 = jnp.pad(w_in[l, :, f_col:f_col + n_h], ((0, 0), (0, LANES - n_h))).astype(BF16)
        weights = (w_main16, w_f16, conv_w[l], conv_b[l], conv_ln_g[l], conv_ln_b[l], ret_gn_g[l], ret_gn_b[l],
                   fox_bf[l], _toeplitz_row(rel_bias[l]), w_out[l].astype(BF16), ln_g[l], ln_b[l], log_gamma)
        xp32, xp16, sp = _layer(xp32, xp16, None, weights, b_p, t_p, 0, alpha)
        hist = (cache_conv[l], state_ret[l], cache_fox_k, cache_fox_v, cache_fox_logf[l],
                cache_band_k, cache_band_v, l)
        xs32, xs16, ss = _layer(xs32, xs16, hist, weights, b_s, t_s, past, alpha)
        st_p.append(sp)
        st_s.append(ss)

    def stack(states, i):
        return jnp.stack([s[i] for s in states], axis=0)

    return (xp32.reshape(b_p, t_p, d_model), xs32.reshape(b_s, t_s, d_model),
            *[stack(st_p, i) for i in range(7)], *[stack(st_s, i) for i in range(7)])
```

```python
import functools

import numpy as np
import jax
import jax.numpy as jnp
from jax import lax
from jax.experimental import pallas as pl
from jax.experimental.pallas import tpu as pltpu

HEAD_DIM = 128
CHUNK = 64
CONV_WIDTH = 31
BAND_CHUNKS = 8
BAND_PAST = BAND_CHUNKS * CHUNK
REL_CLIP = 128
ROPE_BASE = 10000.0
LN_EPS = 1e-5
SCALE = HEAD_DIM ** -0.5
LOG2E = 1.4426950408889634

F32 = jnp.float32
BF16 = jnp.bfloat16
NEG_INF = float("-inf")

VMEM_LIMIT_BYTES = 52 * 1024 * 1024
LANES = 128
SUBLANES = 8
CONV_HALO = 32
BAND_QTILE = 2 * CHUNK
BAND_KBLOCKS = (BAND_PAST + BAND_QTILE) // BAND_QTILE
TOEPLITZ_WIDTH = 768
FOX_VROWS = HEAD_DIM + 16
OUTPROJ_TN = 1024
OUTPROJ_MCHUNK = 128

_NT = (((1,), (1,)), ((), ()))
_TN = (((0,), (0,)), ((), ()))


def _params(*sem):
    return pltpu.CompilerParams(dimension_semantics=sem, vmem_limit_bytes=VMEM_LIMIT_BYTES)


def _silu(x):
    return x * jax.nn.sigmoid(x)


def _log_sigmoid(x):
    return jnp.minimum(x, 0.0) - jnp.log1p(jnp.exp(-jnp.abs(x)))


def _inproj_kernel(x_ref, w_ref, o32_ref, o16_ref):
    acc = jnp.dot(x_ref[...], w_ref[...], preferred_element_type=F32)
    o32_ref[...] = acc
    o16_ref[...] = acc.astype(BF16)


def _inproj(x16, w16, tm, tn):
    m, k = x16.shape
    n = w16.shape[1]
    return pl.pallas_call(
        _inproj_kernel,
        grid=(m // tm, n // tn),
        in_specs=[pl.BlockSpec((tm, k), lambda i, j: (i, 0)),
                  pl.BlockSpec((k, tn), lambda i, j: (0, j))],
        out_specs=[pl.BlockSpec((tm, tn), lambda i, j: (i, j)),
                   pl.BlockSpec((tm, tn), lambda i, j: (i, j))],
        out_shape=[jax.ShapeDtypeStruct((m, n), F32), jax.ShapeDtypeStruct((m, n), BF16)],
        compiler_params=_params("parallel", "arbitrary"),
    )(x16, w16)


def _forget_kernel(x_ref, w_ref, o_ref):
    o_ref[...] = jnp.dot(x_ref[...], w_ref[...], preferred_element_type=F32)


def _forget_logits(x16, wf16, tm):
    m, k = x16.shape
    n = wf16.shape[1]
    return pl.pallas_call(
        _forget_kernel,
        grid=(m // tm,),
        in_specs=[pl.BlockSpec((tm, k), lambda i: (i, 0)),
                  pl.BlockSpec((k, n), lambda i: (0, 0))],
        out_specs=pl.BlockSpec((tm, n), lambda i: (i, 0)),
        out_shape=jax.ShapeDtypeStruct((m, n), F32),
        compiler_params=_params("parallel"),
    )(x16, wf16)


def _cumsum_kernel(f_ref, b_ref, init_ref, lf_ref, c_ref, carry_ref, *, apply_log_sigmoid):
    @pl.when(pl.program_id(1) == 0)
    def _():
        carry_ref[...] = init_ref[0]

    x = f_ref[...]
    if apply_log_sigmoid:
        x = _log_sigmoid(x + b_ref[...])
    tt = x.shape[0]
    row = lax.broadcasted_iota(jnp.int32, (tt, tt), 0)
    col = lax.broadcasted_iota(jnp.int32, (tt, tt), 1)
    tri = (col <= row).astype(F32)
    cs = jnp.dot(tri, x, precision=lax.Precision.HIGHEST, preferred_element_type=F32) + carry_ref[...]
    lf_ref[...] = x
    c_ref[...] = cs
    carry_ref[...] = cs[tt - 1:tt, :]


def _cumsum(f, bias, init, batch, tt, apply_log_sigmoid):
    rows, n = f.shape
    nt = rows // batch // tt
    return pl.pallas_call(
        functools.partial(_cumsum_kernel, apply_log_sigmoid=apply_log_sigmoid),
        grid=(batch, nt),
        in_specs=[pl.BlockSpec((tt, n), lambda b, j: (b * nt + j, 0)),
                  pl.BlockSpec((1, n), lambda b, j: (0, 0)),
                  pl.BlockSpec((1, 1, n), lambda b, j: (b, 0, 0))],
        out_specs=[pl.BlockSpec((tt, n), lambda b, j: (b * nt + j, 0)),
                   pl.BlockSpec((tt, n), lambda b, j: (b * nt + j, 0))],
        out_shape=[jax.ShapeDtypeStruct((rows, n), F32), jax.ShapeDtypeStruct((rows, n), F32)],
        scratch_shapes=[pltpu.VMEM((1, n), F32)],
        compiler_params=_params("arbitrary", "arbitrary"),
    )(f, bias, init)


def _conv_kernel(val_ref, glu_ref, gate_ref, buf_ref, w_ref, cb_ref, g_ref, be_ref,
                 out_ref, st_ref, u_ref, y_ref, sh_ref, *, tt, nt):
    j = pl.program_id(1)
    chans = val_ref.shape[1]

    @pl.when(j == 0)
    def _():
        u_ref[0:CONV_HALO, :] = buf_ref[0]

    u_ref[CONV_HALO:CONV_HALO + tt, :] = val_ref[...] * jax.nn.sigmoid(glu_ref[...])
    sh_rows = sh_ref.shape[1]
    for b in range(1, SUBLANES):
        sh_ref[b - 1] = u_ref[b:b + sh_rows, :]
    first = CONV_HALO - (CONV_WIDTH - 1)
    rc = 32
    for r in range(tt // rc):
        acc = jnp.broadcast_to(cb_ref[...], (rc, chans))
        for k in range(CONV_WIDTH):
            a, b = divmod(first + k, SUBLANES)
            lo = r * rc + a * SUBLANES
            src = u_ref[lo:lo + rc, :] if b == 0 else sh_ref[b - 1, lo:lo + rc, :]
            acc = acc + src * w_ref[k:k + 1, :]
        y_ref[r * rc:(r + 1) * rc, :] = acc
    y = y_ref[...]
    mu = jnp.mean(y, axis=-1, keepdims=True)
    d = y - mu
    var = jnp.mean(d * d, axis=-1, keepdims=True)
    yn = d * lax.rsqrt(var + LN_EPS) * g_ref[...] + be_ref[...]
    out_ref[...] = (_silu(yn) * _silu(gate_ref[...])).astype(BF16)

    tail = u_ref[tt:tt + CONV_HALO, :]

    @pl.when(j == nt - 1)
    def _():
        st_ref[0] = tail

    u_ref[0:CONV_HALO, :] = tail


def _conv_branch(z32, buf, conv_w, conv_b, ln_g, ln_b, batch, t_len, width):
    tt = min(t_len, 256)
    nt = t_len // tt
    rows = batch * t_len
    buf32 = jnp.pad(buf.astype(F32), ((0, 0), (CONV_HALO - (CONV_WIDTH - 1), 0), (0, 0)))
    w32 = jnp.pad(conv_w, ((0, CONV_HALO - CONV_WIDTH), (0, 0)))
    row_map = lambda c: (lambda b, j: (b * nt + j, c))
    vec = pl.BlockSpec((1, width), lambda b, j: (0, 0))
    out, st = pl.pallas_call(
        functools.partial(_conv_kernel, tt=tt, nt=nt),
        grid=(batch, nt),
        in_specs=[pl.BlockSpec((tt, width), row_map(0)),
                  pl.BlockSpec((tt, width), row_map(1)),
                  pl.BlockSpec((tt, width), row_map(2)),
                  pl.BlockSpec((1, CONV_HALO, width), lambda b, j: (b, 0, 0)),
                  pl.BlockSpec((CONV_HALO, width), lambda b, j: (0, 0)),
                  vec, vec, vec],
        out_specs=[pl.BlockSpec((tt, width), lambda b, j: (b * nt + j, 0)),
                   pl.BlockSpec((1, CONV_HALO, width), lambda b, j: (b, 0, 0))],
        out_shape=[jax.ShapeDtypeStruct((rows, width), BF16),
                   jax.ShapeDtypeStruct((batch, CONV_HALO, width), F32)],
        scratch_shapes=[pltpu.VMEM((CONV_HALO + tt, width), F32), pltpu.VMEM((tt, width), F32),
                        pltpu.VMEM((SUBLANES - 1, CONV_HALO - SUBLANES + tt, width), F32)],
        compiler_params=_params("arbitrary", "arbitrary"),
    )(z32, z32, z32, buf32, w32, conv_b[None, :], ln_g[None, :], ln_b[None, :])
    return out, st[:, CONV_HALO - (CONV_WIDTH - 1):, :]


def _ret_kernel(lg_ref, q_ref, k_ref, v_ref, g_ref, cos_ref, sin_ref, gg_ref, gb_ref, s0_ref,
                out_ref, sout_ref, s_ref, dec_ref, *, blk, nblk):
    n = pl.program_id(1)
    n_h = s_ref.shape[0]

    @pl.when(n == 0)
    def _():
        s_ref[...] = s0_ref[0]
        i = lax.broadcasted_iota(jnp.int32, (blk, blk), 0)
        j = lax.broadcasted_iota(jnp.int32, (blk, blk), 1)
        diff = i - j
        for h in range(n_h):
            dec_ref[h] = jnp.where(diff >= 0, jnp.exp(jnp.maximum(diff, 0).astype(F32) * lg_ref[h]), 0.0)

    cos = cos_ref[...]
    sin = sin_ref[...]

    def rot(x):
        return x * cos + pltpu.roll(x, HEAD_DIM // 2, 1) * sin

    idx = lax.broadcasted_iota(jnp.int32, (blk, 1), 0).astype(F32)
    for h in range(n_h):
        sl = slice(h * HEAD_DIM, (h + 1) * HEAD_DIM)
        lg = lg_ref[h]
        q = rot(q_ref[:, sl]) * SCALE
        k = rot(k_ref[:, sl])
        v = v_ref[:, sl]
        q_dec = jnp.exp((idx + 1.0) * lg)
        k_dec = jnp.exp((blk - 1.0 - idx) * lg)
        s_dec = jnp.exp(jnp.full((1, HEAD_DIM), blk, F32) * lg)

        state = s_ref[h]
        inner = lax.dot_general(q.astype(BF16), k.astype(BF16), _NT, preferred_element_type=F32) * dec_ref[h]
        o = (jnp.dot(inner.astype(BF16), v, preferred_element_type=F32)
             + jnp.dot((q * q_dec).astype(BF16), state.astype(BF16), preferred_element_type=F32))
        s_ref[h] = s_dec * state + lax.dot_general((k * k_dec).astype(BF16), v, _TN,
                                                   preferred_element_type=F32)

        mu = jnp.mean(o, axis=-1, keepdims=True)
        d = o - mu
        var = jnp.mean(d * d, axis=-1, keepdims=True)
        y = d * lax.rsqrt(var + LN_EPS) * gg_ref[:, sl] + gb_ref[:, sl]
        out_ref[:, sl] = (y * _silu(g_ref[:, sl])).astype(BF16)

    @pl.when(n == nblk - 1)
    def _():
        sout_ref[0] = s_ref[...]


def _retention(z32, z16, cos2, sin2, gn_g, gn_b, s0, log_gamma, batch, t_len, blk, col0):
    n_h = s0.shape[1]
    width = n_h * HEAD_DIM
    nblk = t_len // blk
    rows = batch * t_len
    col = lambda c: (lambda b, n: (b * nblk + n, c))
    blk_spec = lambda c: pl.BlockSpec((blk, width), col(c))
    tab = pl.BlockSpec((blk, HEAD_DIM), lambda b, n: (n, 0))
    vec = pl.BlockSpec((1, width), lambda b, n: (0, 0))
    st = pl.BlockSpec((1, n_h, HEAD_DIM, HEAD_DIM), lambda b, n: (b, 0, 0, 0))
    return pl.pallas_call(
        functools.partial(_ret_kernel, blk=blk, nblk=nblk),
        grid=(batch, nblk),
        in_specs=[pl.BlockSpec(memory_space=pltpu.SMEM),
                  blk_spec(col0), blk_spec(col0 + 1), blk_spec(col0 + 2), blk_spec(col0 + 3),
                  tab, tab, vec, vec, st],
        out_specs=[blk_spec(0), st],
        out_shape=[jax.ShapeDtypeStruct((rows, width), BF16),
                   jax.ShapeDtypeStruct(s0.shape, F32)],
        scratch_shapes=[pltpu.VMEM((n_h, HEAD_DIM, HEAD_DIM), F32), pltpu.VMEM((n_h, blk, blk), F32)],
        compiler_params=_params("arbitrary", "arbitrary"),
    )(log_gamma, z32, z32, z16, z32, cos2, sin2, gn_g[None, :], gn_b[None, :], s0)


def _fox_prompt_kernel(qt_ref, kt_ref, q_ref, k_ref, vt_ref, g_ref, cq_ref, ck_ref, out_ref,
                       m_ref, l_ref, acc_ref, *, n_h):
    p = pl.program_id(1)
    qi = qt_ref[p]
    ki = kt_ref[p]
    tq = q_ref.shape[0]

    @pl.when(ki == 0)
    def _():
        m_ref[...] = jnp.full(m_ref.shape, NEG_INF, F32)
        l_ref[...] = jnp.zeros(l_ref.shape, F32)
        acc_ref[...] = jnp.zeros(acc_ref.shape, F32)

    def accumulate(diagonal):
        ck2 = ck_ref[...] * LOG2E
        cq2 = cq_ref[0] * LOG2E
        if diagonal:
            key = lax.broadcasted_iota(jnp.int32, (tq, tq), 0)
            qry = lax.broadcasted_iota(jnp.int32, (tq, tq), 1)
            visible = key <= qry
        for h in range(n_h):
            sl = slice(h * HEAD_DIM, (h + 1) * HEAD_DIM)
            t = lax.dot_general(k_ref[:, sl], q_ref[:, sl], _NT, preferred_element_type=F32) * (SCALE * LOG2E)
            t = t - ck2[:, h:h + 1]
            if diagonal:
                t = jnp.where(visible, t, NEG_INF)
            cq_h = cq2[h:h + 1, :]
            m_prev = m_ref[h]
            m_new = jnp.maximum(m_prev, cq_h + jnp.max(t, axis=0, keepdims=True))
            alpha = jnp.exp2(m_prev - m_new)
            prob = jnp.exp2(t - (m_new - cq_h))
            vsl = slice(h * FOX_VROWS, (h + 1) * FOX_VROWS)
            acc_ref[h] = alpha * acc_ref[h] + jnp.dot(vt_ref[0, vsl, :], prob.astype(BF16),
                                                      preferred_element_type=F32)
            m_ref[h] = m_new

    @pl.when(ki < qi)
    def _():
        accumulate(False)

    @pl.when(ki == qi)
    def _():
        accumulate(True)
        for h in range(n_h):
            sl = slice(h * HEAD_DIM, (h + 1) * HEAD_DIM)
            o = (acc_ref[h, 0:HEAD_DIM, :] / acc_ref[h, HEAD_DIM:HEAD_DIM + 1, :]).T
            out_ref[:, sl] = (o * _silu(g_ref[:, sl])).astype(BF16)


def _fox_prompt(z32, z16, c, c_t, batch, t_len, n_h, col0, tq):
    nq = t_len // tq
    width = n_h * HEAD_DIM
    rows = batch * t_len
    v_heads = z16[:, (col0 + 2) * width:(col0 + 3) * width].reshape(batch, t_len, n_h, HEAD_DIM)
    v_heads = jnp.pad(v_heads, ((0, 0), (0, 0), (0, 0), (0, FOX_VROWS - HEAD_DIM)), constant_values=1.0)
    v_t = jnp.swapaxes(v_heads.reshape(batch, t_len, n_h * FOX_VROWS), 1, 2)
    pairs =[(a, b) for a in range(nq) for b in range(a + 1)]
    q_tab = jnp.asarray(np.array([a for a, _ in pairs], np.int32))
    k_tab = jnp.asarray(np.array([b for _, b in pairs], np.int32))
    qmap = lambda c_: (lambda b, p, qt, kt: (b * nq + qt[p], c_))
    kmap = lambda c_: (lambda b, p, qt, kt: (b * nq + kt[p], c_))
    grid_spec = pltpu.PrefetchScalarGridSpec(
        num_scalar_prefetch=2,
        grid=(batch, len(pairs)),
        in_specs=[pl.BlockSpec((tq, width), qmap(col0)),
                  pl.BlockSpec((tq, width), kmap(col0 + 1)),
                  pl.BlockSpec((1, n_h * FOX_VROWS, tq), lambda b, p, qt, kt: (b, 0, kt[p])),
                  pl.BlockSpec((tq, width), qmap(col0 + 3)),
                  pl.BlockSpec((1, n_h, tq), lambda b, p, qt, kt: (b, 0, qt[p])),
                  pl.BlockSpec((tq, LANES), kmap(0))],
        out_specs=pl.BlockSpec((tq, width), qmap(0)),
        scratch_shapes=[pltpu.VMEM((n_h, 1, tq), F32), pltpu.VMEM((n_h, 1, tq), F32),
                        pltpu.VMEM((n_h, FOX_VROWS, tq), F32)],
    )
    return pl.pallas_call(
        functools.partial(_fox_prompt_kernel, n_h=n_h),
        grid_spec=grid_spec,
        out_shape=jax.ShapeDtypeStruct((rows, width), BF16),
        compiler_params=_params("arbitrary", "arbitrary"),
    )(q_tab, k_tab, z16, z16, v_t, z32, c_t, c)


def _fox_sample_kernel(q_ref, kc_ref, vc_ref, kn_ref, vn_ref, g_ref, cq_ref, ckc_ref, ckn_ref, out_ref, *, n_h):
    t_q = q_ref.shape[0]
    row = lax.broadcasted_iota(jnp.int32, (t_q, t_q), 0)
    col = lax.broadcasted_iota(jnp.int32, (t_q, t_q), 1)
    causal = col <= row
    for h in range(n_h):
        sl = slice(h * HEAD_DIM, (h + 1) * HEAD_DIM)
        q = q_ref[:, sl]
        cq = cq_ref[0, h]
        s_c = lax.dot_general(q, kc_ref[:, h, :].astype(BF16), _NT, preferred_element_type=F32) * SCALE
        s_c = s_c + cq - ckc_ref[0, h]
        s_n = lax.dot_general(q, kn_ref[:, sl], _NT, preferred_element_type=F32) * SCALE
        s_n = jnp.where(causal, s_n + cq - ckn_ref[0, h], NEG_INF)
        m = jnp.maximum(jnp.max(s_c, axis=-1, keepdims=True), jnp.max(s_n, axis=-1, keepdims=True))
        p_c = jnp.exp(s_c - m)
        p_n = jnp.exp(s_n - m)
        denom = jnp.sum(p_c, axis=-1, keepdims=True) + jnp.sum(p_n, axis=-1, keepdims=True)
        o = (jnp.dot(p_c.astype(BF16), vc_ref[:, h, :].astype(BF16), preferred_element_type=F32)
             + jnp.dot(p_n.astype(BF16), vn_ref[:, sl], preferred_element_type=F32)) / denom
        out_ref[:, sl] = (o * _silu(g_ref[:, sl])).astype(BF16)


def _fox_sample(z32, z16, k_cache, v_cache, layer, cq, ckc, ckn, batch, t_len, n_h, col0):
    past = k_cache.shape[2]
    rows = batch * t_len
    width = n_h * HEAD_DIM
    col = lambda c: (lambda b: (b, c))
    cache = pl.BlockSpec((None, None, past, n_h, HEAD_DIM), lambda b: (layer, b, 0, 0, 0))
    blk = lambda m: pl.BlockSpec((t_len, width), m)
    return pl.pallas_call(
        functools.partial(_fox_sample_kernel, n_h=n_h),
        grid=(batch,),
        in_specs=[blk(col(col0)), cache, cache, blk(col(col0 + 1)), blk(col(col0 + 2)), blk(col(col0 + 3)),
                  pl.BlockSpec((1, n_h, t_len, 1), lambda b: (b, 0, 0, 0)),
                  pl.BlockSpec((1, n_h, 1, past), lambda b: (b, 0, 0, 0)),
                  pl.BlockSpec((1, n_h, 1, t_len), lambda b: (b, 0, 0, 0))],
        out_specs=blk(col(0)),
        out_shape=jax.ShapeDtypeStruct((rows, width), BF16),
        compiler_params=_params("parallel"),
    )(z16, k_cache, v_cache, z16, z16, z32, cq, ckc, ckn)


def _toeplitz_row(table):
    n_heads = table.shape[0]
    used = BAND_PAST + BAND_QTILE
    far = table[:, 2 * REL_CLIP:]
    row = jnp.concatenate([jnp.broadcast_to(far, (n_heads, BAND_PAST - REL_CLIP)), table[:, ::-1]], axis=1)
    assert row.shape[1] >= used and TOEPLITZ_WIDTH >= used + BAND_QTILE - 1
    tail = jnp.broadcast_to(far, (n_heads, TOEPLITZ_WIDTH - used))
    return jnp.concatenate([row[:, :used], tail], axis=1).astype(F32)


def _toeplitz(row, n_rows):
    return pltpu.roll(jnp.broadcast_to(row, (n_rows, row.shape[1])), 0, 1, stride=1, stride_axis=0)


def _band_prompt_kernel(*refs, n_h):
    nkb = BAND_KBLOCKS
    trev_ref, q_ref = refs[0], refs[1]
    k_refs = refs[2:2 + nkb]
    v_refs = refs[2 + nkb:2 + 2 * nkb]
    g_ref, out_ref, bias_ref = refs[2 + 2 * nkb:]
    qi = pl.program_id(1)
    tq = BAND_QTILE
    n_keys = nkb * tq

    @pl.when(jnp.logical_and(pl.program_id(0) == 0, qi == 0))
    def _():
        qc = lax.broadcasted_iota(jnp.int32, (tq, n_keys), 0) // CHUNK
        kc = lax.broadcasted_iota(jnp.int32, (tq, n_keys), 1) // CHUNK
        visible = jnp.logical_and(kc >= qc, kc <= qc + BAND_CHUNKS)
        for h in range(n_h):
            bias = _toeplitz(trev_ref[h:h + 1, :], tq)[:, :n_keys]
            bias_ref[h] = jnp.where(visible, bias, NEG_INF).T

    for h in range(n_h):
        sl = slice(h * HEAD_DIM, (h + 1) * HEAD_DIM)
        q = q_ref[:, sl]
        scores = []
        for jb in range(nkb):
            s = lax.dot_general(k_refs[jb][:, sl], q, _NT, preferred_element_type=F32) * SCALE
            s = s + bias_ref[h, jb * tq:(jb + 1) * tq, :]
            scores.append(jnp.where(qi + jb >= nkb - 1, s, NEG_INF))
        m = functools.reduce(jnp.maximum, [jnp.max(s, axis=0, keepdims=True) for s in scores])
        probs = [jnp.exp(s - m) for s in scores]
        denom = sum(jnp.sum(p, axis=0, keepdims=True) for p in probs)
        o_t = sum(jnp.dot(v_refs[jb][0, sl, :], p.astype(BF16), preferred_element_type=F32)
                  for jb, p in enumerate(probs)) / denom
        out_ref[:, sl] = (o_t.T * _silu(g_ref[:, sl])).astype(BF16)


def _band_prompt(z32, z16, trev, batch, t_len, n_h, col0):
    tq = BAND_QTILE
    nq = t_len // tq
    nkb = BAND_KBLOCKS
    width = n_h * HEAD_DIM
    rows = batch * t_len
    qmap = lambda c: (lambda b, i: (b * nq + i, c))
    kmap = lambda c, jb: (lambda b, i: (b * nq + jnp.maximum(i + jb - (nkb - 1), 0), c))
    blk = lambda m: pl.BlockSpec((tq, width), m)
    v_t = jnp.swapaxes(z16[:, (col0 + 2) * width:(col0 + 3) * width].reshape(batch, t_len, width), 1, 2)
    vmap = lambda jb: (lambda b, i: (b, 0, jnp.maximum(i + jb - (nkb - 1), 0)))
    in_specs = ([pl.BlockSpec(trev.shape, lambda b, i: (0, 0)), blk(qmap(col0))]
                + [blk(kmap(col0 + 1, jb)) for jb in range(nkb)]
                + [pl.BlockSpec((1, width, tq), vmap(jb)) for jb in range(nkb)]
                + [blk(qmap(col0 + 3))])
    return pl.pallas_call(
        functools.partial(_band_prompt_kernel, n_h=n_h),
        grid=(batch, nq),
        in_specs=in_specs,
        out_specs=blk(qmap(0)),
        out_shape=jax.ShapeDtypeStruct((rows, width), BF16),
        scratch_shapes=[pltpu.VMEM((n_h, nkb * tq, tq), F32)],
        compiler_params=_params("arbitrary", "arbitrary"),
    )(trev, z16, *([z16] * nkb), *([v_t] * nkb), z32)


def _band_sample_kernel(trev_ref, q_ref, kc_ref, vc_ref, kn_ref, vn_ref, g_ref, out_ref, bias_ref,
                        *, n_h, past):
    t_q = q_ref.shape[0]
    n_cache = kc_ref.shape[0]
    width = bias_ref.shape[2]

    @pl.when(pl.program_id(0) == 0)
    def _():
        i = lax.broadcasted_iota(jnp.int32, (t_q, width), 0)
        j = lax.broadcasted_iota(jnp.int32, (t_q, width), 1)
        q_pos = past + i
        k_pos = past - n_cache + j
        qc = q_pos // CHUNK
        kc = k_pos // CHUNK
        visible = jnp.logical_and(jnp.logical_and(k_pos >= 0, j < n_cache + t_q),
                                  jnp.logical_and(kc <= qc, kc >= qc - BAND_CHUNKS))
        for h in range(n_h):
            bias_ref[h] = jnp.where(visible, _toeplitz(trev_ref[h:h + 1, :], t_q), NEG_INF)

    for h in range(n_h):
        sl = slice(h * HEAD_DIM, (h + 1) * HEAD_DIM)
        q = q_ref[:, sl]
        s_c = lax.dot_general(q, kc_ref[:, h, :].astype(BF16), _NT, preferred_element_type=F32) * SCALE
        s_c = s_c + bias_ref[h, :, 0:n_cache]
        s_n = lax.dot_general(q, kn_ref[:, sl], _NT, preferred_element_type=F32) * SCALE
        s_n = s_n + bias_ref[h, :, n_cache:n_cache + t_q]
        m = jnp.maximum(jnp.max(s_c, axis=-1, keepdims=True), jnp.max(s_n, axis=-1, keepdims=True))
        p_c = jnp.exp(s_c - m)
        p_n = jnp.exp(s_n - m)
        denom = jnp.sum(p_c, axis=-1, keepdims=True) + jnp.sum(p_n, axis=-1, keepdims=True)
        o = (jnp.dot(p_c.astype(BF16), vc_ref[:, h, :].astype(BF16), preferred_element_type=F32)
             + jnp.dot(p_n.astype(BF16), vn_ref[:, sl], preferred_element_type=F32)) / denom
        out_ref[:, sl] = (o * _silu(g_ref[:, sl])).astype(BF16)


def _band_sample(z32, z16, k_cache, v_cache, layer, trev, batch, t_len, n_h, col0, past):
    width = n_h * HEAD_DIM
    rows = batch * t_len
    n_cache = k_cache.shape[2]
    assert n_cache == BAND_PAST and past >= BAND_PAST and t_len <= BAND_QTILE
    col = lambda c: (lambda b: (b, c))
    cache = pl.BlockSpec((None, None, n_cache, n_h, HEAD_DIM), lambda b: (layer, b, 0, 0, 0))
    blk = lambda m: pl.BlockSpec((t_len, width), m)
    return pl.pallas_call(
        functools.partial(_band_sample_kernel, n_h=n_h, past=past),
        grid=(batch,),
        in_specs=[pl.BlockSpec(trev.shape, lambda b: (0, 0)), blk(col(col0)), cache, cache,
                  blk(col(col0 + 1)), blk(col(col0 + 2)), blk(col(col0 + 3))],
        out_specs=blk(col(0)),
        out_shape=jax.ShapeDtypeStruct((rows, width), BF16),
        scratch_shapes=[pltpu.VMEM((n_h, t_len, TOEPLITZ_WIDTH), F32)],
        compiler_params=_params("arbitrary"),
    )(trev, z16, k_cache, v_cache, z16, z16, z32)


def _outproj_kernel(m0, m1, m2, m3, w_ref, x_ref, g_ref, b_ref, y_ref, y16_ref, r_ref, mu_ref, rs_ref,
                    *, alpha, n_tiles):
    i = pl.program_id(0)
    j = pl.program_id(1)
    branches = (m0, m1, m2, m3)
    width = m0.shape[1]
    tn = w_ref.shape[1]
    n_col = r_ref.shape[2] // tn
    cur = i % 2
    prev = 1 - cur

    @pl.when(jnp.logical_and(i >= 1, j == 0))
    def _():
        for r0 in range(0, r_ref.shape[1], OUTPROJ_MCHUNK):
            rows = slice(r0, min(r0 + OUTPROJ_MCHUNK, r_ref.shape[1]))
            r = r_ref[prev, rows, :]
            mu = jnp.mean(r, axis=-1, keepdims=True)
            d = r - mu
            mu_ref[rows, :] = mu
            rs_ref[rows, :] = lax.rsqrt(jnp.mean(d * d, axis=-1, keepdims=True) + LN_EPS)

    for jj in range(n_col):
        cols = slice(jj * tn, (jj + 1) * tn)

        @pl.when(jnp.logical_and(i < n_tiles, j == jj))
        def _(cols=cols):
            acc = alpha * x_ref[...]
            for a in range(4):
                acc = acc + jnp.dot(branches[a][...], w_ref[a * width:(a + 1) * width, :],
                                    preferred_element_type=F32)
            r_ref[cur, :, cols] = acc

        @pl.when(jnp.logical_and(i >= 1, j == jj))
        def _(cols=cols):
            y = (r_ref[prev, :, cols] - mu_ref[...]) * rs_ref[...] * g_ref[:, cols] + b_ref[:, cols]
            y_ref[...] = y
            y16_ref[...] = y.astype(BF16)


def _outproj(branches, w16, x32, ln_g, ln_b, alpha, tm):
    m, d_model = x32.shape
    width = branches[0].shape[1]
    tn = OUTPROJ_TN
    n_col = d_model // tn
    n_tiles = m // tm
    in_row = lambda i: jnp.minimum(i, n_tiles - 1)
    out_map = lambda i, j: (jnp.maximum(i - 1, 0), jnp.where(i == 0, 0, j))
    vec = pl.BlockSpec((1, d_model), lambda i, j: (0, 0))
    return pl.pallas_call(
        functools.partial(_outproj_kernel, alpha=alpha, n_tiles=n_tiles),
        grid=(n_tiles + 1, n_col),
        in_specs=[pl.BlockSpec((tm, width), lambda i, j: (in_row(i), 0), pipeline_mode=pl.Buffered(1))
                  for _ in range(4)]
                 + [pl.BlockSpec((4 * width, tn), lambda i, j: (0, j)),
                    pl.BlockSpec((tm, tn), lambda i, j: (in_row(i), j)), vec, vec],
        out_specs=[pl.BlockSpec((tm, tn), out_map), pl.BlockSpec((tm, tn), out_map)],
        out_shape=[jax.ShapeDtypeStruct((m, d_model), F32), jax.ShapeDtypeStruct((m, d_model), BF16)],
        scratch_shapes=[pltpu.VMEM((2, tm, d_model), F32), pltpu.VMEM((tm, 1), F32), pltpu.VMEM((tm, 1), F32)],
        compiler_params=_params("arbitrary", "arbitrary"),
    )(*branches, w16, x32, ln_g[None, :], ln_b[None, :])


def _rope_tables(pos0, t_len):
    half = HEAD_DIM // 2
    inv = ROPE_BASE ** (-jnp.arange(half, dtype=F32) / half)
    ang = (pos0 + jnp.arange(t_len)).astype(F32)[:, None] * inv[None, :]
    cos = jnp.cos(ang)
    sin = jnp.sin(ang)
    return jnp.concatenate([cos, cos], axis=1), jnp.concatenate([-sin, sin], axis=1)


def _head_major(c, batch, t_len, n_h):
    return jnp.swapaxes(c.reshape(batch, t_len, -1)[:, :, :n_h], 1, 2)


def _layer(x32, x16, hist, weights, batch, t_len, pos0, alpha):
    (w_main16, w_f16, conv_w, conv_b, conv_ln_g, conv_ln_b, ret_gn_g, ret_gn_b, fox_bf,
     trev, w_out16, ln_g, ln_b, log_gamma) = weights
    width = conv_w.shape[1]
    n_h = width // HEAD_DIM
    rows = batch * t_len
    prompt = hist is None
    tm = min(rows, 512)

    z32, z16 = _inproj(x16, w_main16, min(rows, 1024), 1024)
    f_logits = _forget_logits(x16, w_f16, tm)
    bias_f = jnp.pad(fox_bf, (0, LANES - n_h))[None, :]
    zero_row = jnp.zeros((batch, 1, LANES), F32)

    if prompt:
        conv_buf = jnp.zeros((batch, CONV_WIDTH - 1, width), F32)
        ret_s0 = jnp.zeros((batch, n_h, HEAD_DIM, HEAD_DIM), F32)
        ret_blk = min(t_len, 256)
    else:
        conv_buf, ret_s0, fox_k_c, fox_v_c, fox_lf_c, band_k_c, band_v_c, layer = hist
        ret_blk = t_len

    mix_a, conv_state = _conv_branch(z32, conv_buf, conv_w, conv_b, conv_ln_g, conv_ln_b, batch, t_len, width)

    cos2, sin2 = _rope_tables(pos0, t_len)
    mix_b, ret_state = _retention(z32, z16, cos2, sin2, ret_gn_g, ret_gn_b, ret_s0.astype(F32), log_gamma,
                                  batch, t_len, ret_blk, 3)

    if prompt:
        logf, c = _cumsum(f_logits, bias_f, zero_row, batch, min(t_len, 256), True)
        mix_c = _fox_prompt(z32, z16, c, _head_major(c, batch, t_len, n_h), batch, t_len, n_h, 7,
                            min(t_len, 1024))
    else:
        past = fox_k_c.shape[2]
        lf_c = jnp.pad(fox_lf_c.astype(F32), ((0, 0), (0, 0), (0, LANES - n_h))).reshape(batch * past, LANES)
        _, c_cache = _cumsum(lf_c, bias_f, zero_row, batch, min(past, 256), False)
        init = c_cache.reshape(batch, past, LANES)[:, past - 1:, :]
        logf, c_new = _cumsum(f_logits, bias_f, init, batch, t_len, True)
        cq = _head_major(c_new, batch, t_len, n_h)[..., None]
        ckc = _head_major(c_cache, batch, past, n_h)[:, :, None, :]
        ckn = _head_major(c_new, batch, t_len, n_h)[:, :, None, :]
        mix_c = _fox_sample(z32, z16, fox_k_c, fox_v_c, layer, cq, ckc, ckn, batch, t_len, n_h, 7)

    if prompt:
        mix_d = _band_prompt(z32, z16, trev, batch, t_len, n_h, 11)
    else:
        mix_d = _band_sample(z32, z16, band_k_c, band_v_c, layer, trev, batch, t_len, n_h, 11, pos0)

    y32, y16 = _outproj((mix_a, mix_b, mix_c, mix_d), w_out16, x32, ln_g, ln_b, alpha, min(rows, 512))

    def heads(c0):
        return z32[:, c0 * width:(c0 + 1) * width].reshape(batch, t_len, n_h, HEAD_DIM)

    fox_k, fox_v = heads(8), heads(9)
    band_k, band_v = heads(12), heads(13)
    if prompt:
        keep = min(BAND_PAST, t_len)
        band_k, band_v = band_k[:, t_len - keep:], band_v[:, t_len - keep:]
    logf = logf.reshape(batch, t_len, LANES)[:, :, :n_h]
    return y32, y16, (conv_state, ret_state, fox_k, fox_v, logf, band_k, band_v)


def kernel(x_prompt, x_sample, cache_conv, state_ret, cache_fox_k, cache_fox_v, cache_fox_logf,
           cache_band_k, cache_band_v, w_in, conv_w, conv_b, conv_ln_g, conv_ln_b, ret_gn_g, ret_gn_b,
           fox_bf, rel_bias, w_out, ln_g, ln_b):
    depth = w_in.shape[0]
    alpha = (2.0 * depth) ** 0.25
    b_p, t_p, d_model = x_prompt.shape
    b_s, t_s, _ = x_sample.shape
    past = cache_fox_k.shape[2]
    width = conv_w.shape[2]
    n_h = width // HEAD_DIM
    f_col = 11 * width
    log_gamma = jnp.asarray(np.log1p(-np.exp2(-5.0 - np.arange(n_h))), F32)

    xp32 = x_prompt.reshape(b_p * t_p, d_model)
    xs32 = x_sample.reshape(b_s * t_s, d_model)
    xp16, xs16 = xp32.astype(BF16), xs32.astype(BF16)
    st_p, st_s = [], []
    for l in range(depth):
        w_main16 = jnp.concatenate([w_in[l, :, :f_col], w_in[l, :, f_col + n_h:]], axis=1).astype(BF16)
        w_f16 = jnp.pad(w_in[l, :, f_col:f_col + n_h], ((0, 0), (0, LANES - n_h))).astype(BF16)
        weights = (w_main16, w_f16, conv_w[l], conv_b[l], conv_ln_g[l], conv_ln_b[l], ret_gn_g[l], ret_gn_b[l],
                   fox_bf[l], _toeplitz_row(rel_bias[l]), w_out[l].astype(BF16), ln_g[l], ln_b[l], log_gamma)
        xp32, xp16, sp = _layer(xp32, xp16, None, weights, b_p, t_p, 0, alpha)
        hist = (cache_conv[l], state_ret[l], cache_fox_k, cache_fox_v, cache_fox_logf[l],
                cache_band_k, cache_band_v, l)
        xs32, xs16, ss = _layer(xs32, xs16, hist, weights, b_s, t_s, past, alpha)
        st_p.append(sp)
        st_s.append(ss)

    def stack(states, i):
        return jnp.stack([s[i] for s in states], axis=0)

    return (xp32.reshape(b_p, t_p, d_model), xs32.reshape(b_s, t_s, d_model),
            *[stack(st_p, i) for i in range(7)], *[stack(st_s, i) for i in range(7)])
```

```python
import functools

import numpy as np
import jax
import jax.numpy as jnp
from jax import lax
from jax.experimental import pallas as pl
from jax.experimental.pallas import tpu as pltpu

HEAD_DIM = 128
CHUNK = 64
CONV_WIDTH = 31
BAND_CHUNKS = 8
BAND_PAST = BAND_CHUNKS * CHUNK
REL_CLIP = 128
ROPE_BASE = 10000.0
LN_EPS = 1e-5
SCALE = HEAD_DIM ** -0.5
LOG2E = 1.4426950408889634

F32 = jnp.float32
BF16 = jnp.bfloat16
NEG_INF = float("-inf")

VMEM_LIMIT_BYTES = 52 * 1024 * 1024
LANES = 128
SUBLANES = 8
CONV_HALO = 32
BAND_QTILE = 2 * CHUNK
BAND_KBLOCKS = (BAND_PAST + BAND_QTILE) // BAND_QTILE
TOEPLITZ_WIDTH = 768
FOX_VROWS = HEAD_DIM + 16
OUTPROJ_TN = 1024
OUTPROJ_MCHUNK = 128

_NT = (((1,), (1,)), ((), ()))
_TN = (((0,), (0,)), ((), ()))


def _params(*sem):
    return pltpu.CompilerParams(dimension_semantics=sem, vmem_limit_bytes=VMEM_LIMIT_BYTES)


def _silu(x):
    return x * jax.nn.sigmoid(x)


def _log_sigmoid(x):
    return jnp.minimum(x, 0.0) - jnp.log1p(jnp.exp(-jnp.abs(x)))


def _inproj_kernel(x_ref, w_ref, o32_ref, o16_ref):
    acc = jnp.dot(x_ref[...], w_ref[...], preferred_element_type=F32)
    o32_ref[...] = acc
    o16_ref[...] = acc.astype(BF16)


def _inproj(x16, w16, tm, tn):
    m, k = x16.shape
    n = w16.shape[1]
    return pl.pallas_call(
        _inproj_kernel,
        grid=(m // tm, n // tn),
        in_specs=[pl.BlockSpec((tm, k), lambda i, j: (i, 0)),
                  pl.BlockSpec((k, tn), lambda i, j: (0, j))],
        out_specs=[pl.BlockSpec((tm, tn), lambda i, j: (i, j)),
                   pl.BlockSpec((tm, tn), lambda i, j: (i, j))],
        out_shape=[jax.ShapeDtypeStruct((m, n), F32), jax.ShapeDtypeStruct((m, n), BF16)],
        compiler_params=_params("parallel", "arbitrary"),
    )(x16, w16)


def _forget_kernel(x_ref, w_ref, o_ref):
    o_ref[...] = jnp.dot(x_ref[...], w_ref[...], preferred_element_type=F32)


def _forget_logits(x16, wf16, tm):
    m, k = x16.shape
    n = wf16.shape[1]
    return pl.pallas_call(
        _forget_kernel,
        grid=(m // tm,),
        in_specs=[pl.BlockSpec((tm, k), lambda i: (i, 0)),
                  pl.BlockSpec((k, n), lambda i: (0, 0))],
        out_specs=pl.BlockSpec((tm, n), lambda i: (i, 0)),
        out_shape=jax.ShapeDtypeStruct((m, n), F32),
        compiler_params=_params("parallel"),
    )(x16, wf16)


def _cumsum_kernel(f_ref, b_ref, init_ref, lf_ref, c_ref, carry_ref, *, apply_log_sigmoid):
    @pl.when(pl.program_id(1) == 0)
    def _():
        carry_ref[...] = init_ref[0]

    x = f_ref[...]
    if apply_log_sigmoid:
        x = _log_sigmoid(x + b_ref[...])
    tt = x.shape[0]
    row = lax.broadcasted_iota(jnp.int32, (tt, tt), 0)
    col = lax.broadcasted_iota(jnp.int32, (tt, tt), 1)
    tri = (col <= row).astype(F32)
    cs = jnp.dot(tri, x, precision=lax.Precision.HIGHEST, preferred_element_type=F32) + carry_ref[...]
    lf_ref[...] = x
    c_ref[...] = cs
    carry_ref[...] = cs[tt - 1:tt, :]


def _cumsum(f, bias, init, batch, tt, apply_log_sigmoid):
    rows, n = f.shape
    nt = rows // batch // tt
    return pl.pallas_call(
        functools.partial(_cumsum_kernel, apply_log_sigmoid=apply_log_sigmoid),
        grid=(batch, nt),
        in_specs=[pl.BlockSpec((tt, n), lambda b, j: (b * nt + j, 0)),
                  pl.BlockSpec((1, n), lambda b, j: (0, 0)),
                  pl.BlockSpec((1, 1, n), lambda b, j: (b, 0, 0))],
        out_specs=[pl.BlockSpec((tt, n), lambda b, j: (b * nt + j, 0)),
                   pl.BlockSpec((tt, n), lambda b, j: (b * nt + j, 0))],
        out_shape=[jax.ShapeDtypeStruct((rows, n), F32), jax.ShapeDtypeStruct((rows, n), F32)],
        scratch_shapes=[pltpu.VMEM((1, n), F32)],
        compiler_params=_params("arbitrary", "arbitrary"),
    )(f, bias, init)


def _conv_kernel(val_ref, glu_ref, gate_ref, buf_ref, w_ref, cb_ref, g_ref, be_ref,
                 out_ref, st_ref, u_ref, y_ref, sh_ref, *, tt, nt):
    j = pl.program_id(1)
    chans = val_ref.shape[1]

    @pl.when(j == 0)
    def _():
        u_ref[0:CONV_HALO, :] = buf_ref[0]

    u_ref[CONV_HALO:CONV_HALO + tt, :] = val_ref[...] * jax.nn.sigmoid(glu_ref[...])
    sh_rows = sh_ref.shape[1]
    for b in range(1, SUBLANES):
        sh_ref[b - 1] = u_ref[b:b + sh_rows, :]
    first = CONV_HALO - (CONV_WIDTH - 1)
    rc = 32
    for r in range(tt // rc):
        acc = jnp.broadcast_to(cb_ref[...], (rc, chans))
        for k in range(CONV_WIDTH):
            a, b = divmod(first + k, SUBLANES)
            lo = r * rc + a * SUBLANES
            src = u_ref[lo:lo + rc, :] if b == 0 else sh_ref[b - 1, lo:lo + rc, :]
            acc = acc + src * w_ref[k:k + 1, :]
        y_ref[r * rc:(r + 1) * rc, :] = acc
    y = y_ref[...]
    mu = jnp.mean(y, axis=-1, keepdims=True)
    d = y - mu
    var = jnp.mean(d * d, axis=-1, keepdims=True)
    yn = d * lax.rsqrt(var + LN_EPS) * g_ref[...] + be_ref[...]
    out_ref[...] = (_silu(yn) * _silu(gate_ref[...])).astype(BF16)

    tail = u_ref[tt:tt + CONV_HALO, :]

    @pl.when(j == nt - 1)
    def _():
        st_ref[0] = tail

    u_ref[0:CONV_HALO, :] = tail


def _conv_branch(z32, buf, conv_w, conv_b, ln_g, ln_b, batch, t_len, width):
    tt = min(t_len, 256)
    nt = t_len // tt
    rows = batch * t_len
    buf32 = jnp.pad(buf.astype(F32), ((0, 0), (CONV_HALO - (CONV_WIDTH - 1), 0), (0, 0)))
    w32 = jnp.pad(conv_w, ((0, CONV_HALO - CONV_WIDTH), (0, 0)))
    row_map = lambda c: (lambda b, j: (b * nt + j, c))
    vec = pl.BlockSpec((1, width), lambda b, j: (0, 0))
    out, st = pl.pallas_call(
        functools.partial(_conv_kernel, tt=tt, nt=nt),
        grid=(batch, nt),
        in_specs=[pl.BlockSpec((tt, width), row_map(0)),
                  pl.BlockSpec((tt, width), row_map(1)),
                  pl.BlockSpec((tt, width), row_map(2)),
                  pl.BlockSpec((1, CONV_HALO, width), lambda b, j: (b, 0, 0)),
                  pl.BlockSpec((CONV_HALO, width), lambda b, j: (0, 0)),
                  vec, vec, vec],
        out_specs=[pl.BlockSpec((tt, width), lambda b, j: (b * nt + j, 0)),
                   pl.BlockSpec((1, CONV_HALO, width), lambda b, j: (b, 0, 0))],
        out_shape=[jax.ShapeDtypeStruct((rows, width), BF16),
                   jax.ShapeDtypeStruct((batch, CONV_HALO, width), F32)],
        scratch_shapes=[pltpu.VMEM((CONV_HALO + tt, width), F32), pltpu.VMEM((tt, width), F32),
                        pltpu.VMEM((SUBLANES - 1, CONV_HALO - SUBLANES + tt, width), F32)],
        compiler_params=_params("arbitrary", "arbitrary"),
    )(z32, z32, z32, buf32, w32, conv_b[None, :], ln_g[None, :], ln_b[None, :])
    return out, st[:, CONV_HALO - (CONV_WIDTH - 1):, :]


def _ret_kernel(lg_ref, q_ref, k_ref, v_ref, g_ref, cos_ref, sin_ref, gg_ref, gb_ref, s0_ref,
                out_ref, sout_ref, s_ref, dec_ref, *, blk, nblk):
    n = pl.program_id(1)
    n_h = s_ref.shape[0]

    @pl.when(n == 0)
    def _():
        s_ref[...] = s0_ref[0]
        i = lax.broadcasted_iota(jnp.int32, (blk, blk), 0)
        j = lax.broadcasted_iota(jnp.int32, (blk, blk), 1)
        diff = i - j
        for h in range(n_h):
            dec_ref[h] = jnp.where(diff >= 0, jnp.exp(jnp.maximum(diff, 0).astype(F32) * lg_ref[h]), 0.0)

    cos = cos_ref[...]
    sin = sin_ref[...]

    def rot(x):
        return x * cos + pltpu.roll(x, HEAD_DIM // 2, 1) * sin

    idx = lax.broadcasted_iota(jnp.int32, (blk, 1), 0).astype(F32)
    for h in range(n_h):
        sl = slice(h * HEAD_DIM, (h + 1) * HEAD_DIM)
        lg = lg_ref[h]
        q = rot(q_ref[:, sl]) * SCALE
        k = rot(k_ref[:, sl])
        v = v_ref[:, sl]
        q_dec = jnp.exp((idx + 1.0) * lg)
        k_dec = jnp.exp((blk - 1.0 - idx) * lg)
        s_dec = jnp.exp(jnp.full((1, HEAD_DIM), blk, F32) * lg)

        state = s_ref[h]
        inner = lax.dot_general(q.astype(BF16), k.astype(BF16), _NT, preferred_element_type=F32) * dec_ref[h]
        o = (jnp.dot(inner.astype(BF16), v, preferred_element_type=F32)
             + jnp.dot((q * q_dec).astype(BF16), state.astype(BF16), preferred_element_type=F32))
        s_ref[h] = s_dec * state + lax.dot_general((k * k_dec).astype(BF16), v, _TN,
                                                   preferred_element_type=F32)

        mu = jnp.mean(o, axis=-1, keepdims=True)
        d = o - mu
        var = jnp.mean(d * d, axis=-1, keepdims=True)
        y = d * lax.rsqrt(var + LN_EPS) * gg_ref[:, sl] + gb_ref[:, sl]
        out_ref[:, sl] = (y * _silu(g_ref[:, sl])).astype(BF16)

    @pl.when(n == nblk - 1)
    def _():
        sout_ref[0] = s_ref[...]


def _retention(z32, z16, cos2, sin2, gn_g, gn_b, s0, log_gamma, batch, t_len, blk, col0):
    n_h = s0.shape[1]
    width = n_h * HEAD_DIM
    nblk = t_len // blk
    rows = batch * t_len
    col = lambda c: (lambda b, n: (b * nblk + n, c))
    blk_spec = lambda c: pl.BlockSpec((blk, width), col(c))
    tab = pl.BlockSpec((blk, HEAD_DIM), lambda b, n: (n, 0))
    vec = pl.BlockSpec((1, width), lambda b, n: (0, 0))
    st = pl.BlockSpec((1, n_h, HEAD_DIM, HEAD_DIM), lambda b, n: (b, 0, 0, 0))
    return pl.pallas_call(
        functools.partial(_ret_kernel, blk=blk, nblk=nblk),
        grid=(batch, nblk),
        in_specs=[pl.BlockSpec(memory_space=pltpu.SMEM),
                  blk_spec(col0), blk_spec(col0 + 1), blk_spec(col0 + 2), blk_spec(col0 + 3),
                  tab, tab, vec, vec, st],
        out_specs=[blk_spec(0), st],
        out_shape=[jax.ShapeDtypeStruct((rows, width), BF16),
                   jax.ShapeDtypeStruct(s0.shape, F32)],
        scratch_shapes=[pltpu.VMEM((n_h, HEAD_DIM, HEAD_DIM), F32), pltpu.VMEM((n_h, blk, blk), F32)],
        compiler_params=_params("arbitrary", "arbitrary"),
    )(log_gamma, z32, z32, z16, z32, cos2, sin2, gn_g[None, :], gn_b[None, :], s0)


def _fox_prompt_kernel(qt_ref, kt_ref, q_ref, k_ref, vt_ref, g_ref, cq_ref, ck_ref, out_ref,
                       m_ref, l_ref, acc_ref, *, n_h):
    p = pl.program_id(1)
    qi = qt_ref[p]
    ki = kt_ref[p]
    tq = q_ref.shape[0]

    @pl.when(ki == 0)
    def _():
        m_ref[...] = jnp.full(m_ref.shape, NEG_INF, F32)
        l_ref[...] = jnp.zeros(l_ref.shape, F32)
        acc_ref[...] = jnp.zeros(acc_ref.shape, F32)

    def accumulate(diagonal):
        ck2 = ck_ref[...] * LOG2E
        cq2 = cq_ref[0] * LOG2E
        if diagonal:
            key = lax.broadcasted_iota(jnp.int32, (tq, tq), 0)
            qry = lax.broadcasted_iota(jnp.int32, (tq, tq), 1)
            visible = key <= qry
        for h in range(n_h):
            sl = slice(h * HEAD_DIM, (h + 1) * HEAD_DIM)
            t = lax.dot_general(k_ref[:, sl], q_ref[:, sl], _NT, preferred_element_type=F32) * (SCALE * LOG2E)
            t = t - ck2[:, h:h + 1]
            if diagonal:
                t = jnp.where(visible, t, NEG_INF)
            cq_h = cq2[h:h + 1, :]
            m_prev = m_ref[h]
            m_new = jnp.maximum(m_prev, cq_h + jnp.max(t, axis=0, keepdims=True))
            alpha = jnp.exp2(m_prev - m_new)
            prob = jnp.exp2(t - (m_new - cq_h))
            vsl = slice(h * FOX_VROWS, (h + 1) * FOX_VROWS)
            acc_ref[h] = alpha * acc_ref[h] + jnp.dot(vt_ref[0, vsl, :], prob.astype(BF16),
                                                      preferred_element_type=F32)
            m_ref[h] = m_new

    @pl.when(ki < qi)
    def _():
        accumulate(False)

    @pl.when(ki == qi)
    def _():
        accumulate(True)
        for h in range(n_h):
            sl = slice(h * HEAD_DIM, (h + 1) * HEAD_DIM)
            o = (acc_ref[h, 0:HEAD_DIM, :] / acc_ref[h, HEAD_DIM:HEAD_DIM + 1, :]).T
            out_ref[:, sl] = (o * _silu(g_ref[:, sl])).astype(BF16)


def _fox_prompt(z32, z16, c, c_t, batch, t_len, n_h, col0, tq):
    nq = t_len // tq
    width = n_h * HEAD_DIM
    rows = batch * t_len
    v_heads = z16[:, (col0 + 2) * width:(col0 + 3) * width].reshape(batch, t_len, n_h, HEAD_DIM)
    v_heads = jnp.pad(v_heads, ((0, 0), (0, 0), (0, 0), (0, FOX_VROWS - HEAD_DIM)), constant_values=1.0)
    v_t = jnp.swapaxes(v_heads.reshape(batch, t_len, n_h * FOX_VROWS), 1, 2)
    pairs =[(a, b) for a in range(nq) for b in range(a + 1)]
    q_tab = jnp.asarray(np.array([a for a, _ in pairs], np.int32))
    k_tab = jnp.asarray(np.array([b for _, b in pairs], np.int32))
    qmap = lambda c_: (lambda b, p, qt, kt: (b * nq + qt[p], c_))
    kmap = lambda c_: (lambda b, p, qt, kt: (b * nq + kt[p], c_))
    grid_spec = pltpu.PrefetchScalarGridSpec(
        num_scalar_prefetch=2,
        grid=(batch, len(pairs)),
        in_specs=[pl.BlockSpec((tq, width), qmap(col0)),
                  pl.BlockSpec((tq, width), kmap(col0 + 1)),
                  pl.BlockSpec((1, n_h * FOX_VROWS, tq), lambda b, p, qt, kt: (b, 0, kt[p])),
                  pl.BlockSpec((tq, width), qmap(col0 + 3)),
                  pl.BlockSpec((1, n_h, tq), lambda b, p, qt, kt: (b, 0, qt[p])),
                  pl.BlockSpec((tq, LANES), kmap(0))],
        out_specs=pl.BlockSpec((tq, width), qmap(0)),
        scratch_shapes=[pltpu.VMEM((n_h, 1, tq), F32), pltpu.VMEM((n_h, 1, tq), F32),
                        pltpu.VMEM((n_h, FOX_VROWS, tq), F32)],
    )
    return pl.pallas_call(
        functools.partial(_fox_prompt_kernel, n_h=n_h),
        grid_spec=grid_spec,
        out_shape=jax.ShapeDtypeStruct((rows, width), BF16),
        compiler_params=_params("arbitrary", "arbitrary"),
    )(q_tab, k_tab, z16, z16, v_t, z32, c_t, c)


def _fox_sample_kernel(q_ref, kc_ref, vc_ref, kn_ref, vn_ref, g_ref, cq_ref, ckc_ref, ckn_ref, out_ref, *, n_h):
    t_q = q_ref.shape[0]
    row = lax.broadcasted_iota(jnp.int32, (t_q, t_q), 0)
    col = lax.broadcasted_iota(jnp.int32, (t_q, t_q), 1)
    causal = col <= row
    for h in range(n_h):
        sl = slice(h * HEAD_DIM, (h + 1) * HEAD_DIM)
        q = q_ref[:, sl]
        cq = cq_ref[0, h]
        s_c = lax.dot_general(q, kc_ref[:, h, :].astype(BF16), _NT, preferred_element_type=F32) * SCALE
        s_c = s_c + cq - ckc_ref[0, h]
        s_n = lax.dot_general(q, kn_ref[:, sl], _NT, preferred_element_type=F32) * SCALE
        s_n = jnp.where(causal, s_n + cq - ckn_ref[0, h], NEG_INF)
        m = jnp.maximum(jnp.max(s_c, axis=-1, keepdims=True), jnp.max(s_n, axis=-1, keepdims=True))
        p_c = jnp.exp(s_c - m)
        p_n = jnp.exp(s_n - m)
        denom = jnp.sum(p_c, axis=-1, keepdims=True) + jnp.sum(p_n, axis=-1, keepdims=True)
        o = (jnp.dot(p_c.astype(BF16), vc_ref[:, h, :].astype(BF16), preferred_element_type=F32)
             + jnp.dot(p_n.astype(BF16), vn_ref[:, sl], preferred_element_type=F32)) / denom
        out_ref[:, sl] = (o * _silu(g_ref[:, sl])).astype(BF16)


def _fox_sample(z32, z16, k_cache, v_cache, layer, cq, ckc, ckn, batch, t_len, n_h, col0):
    past = k_cache.shape[2]
    rows = batch * t_len
    width = n_h * HEAD_DIM
    col = lambda c: (lambda b: (b, c))
    cache = pl.BlockSpec((None, None, past, n_h, HEAD_DIM), lambda b: (layer, b, 0, 0, 0))
    blk = lambda m: pl.BlockSpec((t_len, width), m)
    return pl.pallas_call(
        functools.partial(_fox_sample_kernel, n_h=n_h),
        grid=(batch,),
        in_specs=[blk(col(col0)), cache, cache, blk(col(col0 + 1)), blk(col(col0 + 2)), blk(col(col0 + 3)),
                  pl.BlockSpec((1, n_h, t_len, 1), lambda b: (b, 0, 0, 0)),
                  pl.BlockSpec((1, n_h, 1, past), lambda b: (b, 0, 0, 0)),
                  pl.BlockSpec((1, n_h, 1, t_len), lambda b: (b, 0, 0, 0))],
        out_specs=blk(col(0)),
        out_shape=jax.ShapeDtypeStruct((rows, width), BF16),
        compiler_params=_params("parallel"),
    )(z16, k_cache, v_cache, z16, z16, z32, cq, ckc, ckn)


def _toeplitz_row(table):
    n_heads = table.shape[0]
    used = BAND_PAST + BAND_QTILE
    far = table[:, 2 * REL_CLIP:]
    row = jnp.concatenate([jnp.broadcast_to(far, (n_heads, BAND_PAST - REL_CLIP)), table[:, ::-1]], axis=1)
    assert row.shape[1] >= used and TOEPLITZ_WIDTH >= used + BAND_QTILE - 1
    tail = jnp.broadcast_to(far, (n_heads, TOEPLITZ_WIDTH - used))
    return jnp.concatenate([row[:, :used], tail], axis=1).astype(F32)


def _toeplitz(row, n_rows):
    return pltpu.roll(jnp.broadcast_to(row, (n_rows, row.shape[1])), 0, 1, stride=1, stride_axis=0)


def _band_prompt_kernel(*refs, n_h):
    nkb = BAND_KBLOCKS
    trev_ref, q_ref = refs[0], refs[1]
    k_refs = refs[2:2 + nkb]
    v_refs = refs[2 + nkb:2 + 2 * nkb]
    g_ref, out_ref, bias_ref = refs[2 + 2 * nkb:]
    qi = pl.program_id(1)
    tq = BAND_QTILE
    n_keys = nkb * tq

    @pl.when(jnp.logical_and(pl.program_id(0) == 0, qi == 0))
    def _():
        qc = lax.broadcasted_iota(jnp.int32, (tq, n_keys), 0) // CHUNK
        kc = lax.broadcasted_iota(jnp.int32, (tq, n_keys), 1) // CHUNK
        visible = jnp.logical_and(kc >= qc, kc <= qc + BAND_CHUNKS)
        for h in range(n_h):
            bias = _toeplitz(trev_ref[h:h + 1, :], tq)[:, :n_keys]
            bias_ref[h] = jnp.where(visible, bias, NEG_INF).T

    for h in range(n_h):
        sl = slice(h * HEAD_DIM, (h + 1) * HEAD_DIM)
        q = q_ref[:, sl]
        scores = []
        for jb in range(nkb):
            s = lax.dot_general(k_refs[jb][:, sl], q, _NT, preferred_element_type=F32) * SCALE
            s = s + bias_ref[h, jb * tq:(jb + 1) * tq, :]
            scores.append(jnp.where(qi + jb >= nkb - 1, s, NEG_INF))
        m = functools.reduce(jnp.maximum, [jnp.max(s, axis=0, keepdims=True) for s in scores])
        probs = [jnp.exp(s - m) for s in scores]
        denom = sum(jnp.sum(p, axis=0, keepdims=True) for p in probs)
        o_t = sum(jnp.dot(v_refs[jb][0, sl, :], p.astype(BF16), preferred_element_type=F32)
                  for jb, p in enumerate(probs)) / denom
        out_ref[:, sl] = (o_t.T * _silu(g_ref[:, sl])).astype(BF16)


def _band_prompt(z32, z16, trev, batch, t_len, n_h, col0):
    tq = BAND_QTILE
    nq = t_len // tq
    nkb = BAND_KBLOCKS
    width = n_h * HEAD_DIM
    rows = batch * t_len
    qmap = lambda c: (lambda b, i: (b * nq + i, c))
    kmap = lambda c, jb: (lambda b, i: (b * nq + jnp.maximum(i + jb - (nkb - 1), 0), c))
    blk = lambda m: pl.BlockSpec((tq, width), m)
    v_t = jnp.swapaxes(z16[:, (col0 + 2) * width:(col0 + 3) * width].reshape(batch, t_len, width), 1, 2)
    vmap = lambda jb: (lambda b, i: (b, 0, jnp.maximum(i + jb - (nkb - 1), 0)))
    in_specs = ([pl.BlockSpec(trev.shape, lambda b, i: (0, 0)), blk(qmap(col0))]
                + [blk(kmap(col0 + 1, jb)) for jb in range(nkb)]
                + [pl.BlockSpec((1, width, tq), vmap(jb)) for jb in range(nkb)]
                + [blk(qmap(col0 + 3))])
    return pl.pallas_call(
        functools.partial(_band_prompt_kernel, n_h=n_h),
        grid=(batch, nq),
        in_specs=in_specs,
        out_specs=blk(qmap(0)),
        out_shape=jax.ShapeDtypeStruct((rows, width), BF16),
        scratch_shapes=[pltpu.VMEM((n_h, nkb * tq, tq), F32)],
        compiler_params=_params("arbitrary", "arbitrary"),
    )(trev, z16, *([z16] * nkb), *([v_t] * nkb), z32)


def _band_sample_kernel(trev_ref, q_ref, kc_ref, vc_ref, kn_ref, vn_ref, g_ref, out_ref, bias_ref,
                        *, n_h, past):
    t_q = q_ref.shape[0]
    n_cache = kc_ref.shape[0]
    width = bias_ref.shape[2]

    @pl.when(pl.program_id(0) == 0)
    def _():
        i = lax.broadcasted_iota(jnp.int32, (t_q, width), 0)
        j = lax.broadcasted_iota(jnp.int32, (t_q, width), 1)
        q_pos = past + i
        k_pos = past - n_cache + j
        qc = q_pos // CHUNK
        kc = k_pos // CHUNK
        visible = jnp.logical_and(jnp.logical_and(k_pos >= 0, j < n_cache + t_q),
                                  jnp.logical_and(kc <= qc, kc >= qc - BAND_CHUNKS))
        for h in range(n_h):
            bias_ref[h] = jnp.where(visible, _toeplitz(trev_ref[h:h + 1, :], t_q), NEG_INF)

    for h in range(n_h):
        sl = slice(h * HEAD_DIM, (h + 1) * HEAD_DIM)
        q = q_ref[:, sl]
        s_c = lax.dot_general(q, kc_ref[:, h, :].astype(BF16), _NT, preferred_element_type=F32) * SCALE
        s_c = s_c + bias_ref[h, :, 0:n_cache]
        s_n = lax.dot_general(q, kn_ref[:, sl], _NT, preferred_element_type=F32) * SCALE
        s_n = s_n + bias_ref[h, :, n_cache:n_cache + t_q]
        m = jnp.maximum(jnp.max(s_c, axis=-1, keepdims=True), jnp.max(s_n, axis=-1, keepdims=True))
        p_c = jnp.exp(s_c - m)
        p_n = jnp.exp(s_n - m)
        denom = jnp.sum(p_c, axis=-1, keepdims=True) + jnp.sum(p_n, axis=-1, keepdims=True)
        o = (jnp.dot(p_c.astype(BF16), vc_ref[:, h, :].astype(BF16), preferred_element_type=F32)
             + jnp.dot(p_n.astype(BF16), vn_ref[:, sl], preferred_element_type=F32)) / denom
        out_ref[:, sl] = (o * _silu(g_ref[:, sl])).astype(BF16)


def _band_sample(z32, z16, k_cache, v_cache, layer, trev, batch, t_len, n_h, col0, past):
    width = n_h * HEAD_DIM
    rows = batch * t_len
    n_cache = k_cache.shape[2]
    assert n_cache == BAND_PAST and past >= BAND_PAST and t_len <= BAND_QTILE
    col = lambda c: (lambda b: (b, c))
    cache = pl.BlockSpec((None, None, n_cache, n_h, HEAD_DIM), lambda b: (layer, b, 0, 0, 0))
    blk = lambda m: pl.BlockSpec((t_len, width), m)
    return pl.pallas_call(
        functools.partial(_band_sample_kernel, n_h=n_h, past=past),
        grid=(batch,),
        in_specs=[pl.BlockSpec(trev.shape, lambda b: (0, 0)), blk(col(col0)), cache, cache,
                  blk(col(col0 + 1)), blk(col(col0 + 2)), blk(col(col0 + 3))],
        out_specs=blk(col(0)),
        out_shape=jax.ShapeDtypeStruct((rows, width), BF16),
        scratch_shapes=[pltpu.VMEM((n_h, t_len, TOEPLITZ_WIDTH), F32)],
        compiler_params=_params("arbitrary"),
    )(trev, z16, k_cache, v_cache, z16, z16, z32)


def _outproj_kernel(m0, m1, m2, m3, w_ref, x_ref, g_ref, b_ref, y_ref, y16_ref, r_ref, mu_ref, rs_ref,
                    *, alpha, n_tiles):
    i = pl.program_id(0)
    j = pl.program_id(1)
    branches = (m0, m1, m2, m3)
    width = m0.shape[1]
    tn = w_ref.shape[1]
    n_col = r_ref.shape[2] // tn
    cur = i % 2
    prev = 1 - cur

    @pl.when(jnp.logical_and(i >= 1, j == 0))
    def _():
        for r0 in range(0, r_ref.shape[1], OUTPROJ_MCHUNK):
            rows = slice(r0, min(r0 + OUTPROJ_MCHUNK, r_ref.shape[1]))
            r = r_ref[prev, rows, :]
            mu = jnp.mean(r, axis=-1, keepdims=True)
            d = r - mu
            mu_ref[rows, :] = mu
            rs_ref[rows, :] = lax.rsqrt(jnp.mean(d * d, axis=-1, keepdims=True) + LN_EPS)

    for jj in range(n_col):
        cols = slice(jj * tn, (jj + 1) * tn)

        @pl.when(jnp.logical_and(i < n_tiles, j == jj))
        def _(cols=cols):
            acc = alpha * x_ref[...]
            for a in range(4):
                acc = acc + jnp.dot(branches[a][...], w_ref[a * width:(a + 1) * width, :],
                                    preferred_element_type=F32)
            r_ref[cur, :, cols] = acc

        @pl.when(jnp.logical_and(i >= 1, j == jj))
        def _(cols=cols):
            y = (r_ref[prev, :, cols] - mu_ref[...]) * rs_ref[...] * g_ref[:, cols] + b_ref[:, cols]
            y_ref[...] = y
            y16_ref[...] = y.astype(BF16)


def _outproj(branches, w16, x32, ln_g, ln_b, alpha, tm):
    m, d_model = x32.shape
    width = branches[0].shape[1]
    tn = OUTPROJ_TN
    n_col = d_model // tn
    n_tiles = m // tm
    in_row = lambda i: jnp.minimum(i, n_tiles - 1)
    out_map = lambda i, j: (jnp.maximum(i - 1, 0), jnp.where(i == 0, 0, j))
    vec = pl.BlockSpec((1, d_model), lambda i, j: (0, 0))
    return pl.pallas_call(
        functools.partial(_outproj_kernel, alpha=alpha, n_tiles=n_tiles),
        grid=(n_tiles + 1, n_col),
        in_specs=[pl.BlockSpec((tm, width), lambda i, j: (in_row(i), 0), pipeline_mode=pl.Buffered(1))
                  for _ in range(4)]
                 + [pl.BlockSpec((4 * width, tn), lambda i, j: (0, j)),
                    pl.BlockSpec((tm, tn), lambda i, j: (in_row(i), j)), vec, vec],
        out_specs=[pl.BlockSpec((tm, tn), out_map), pl.BlockSpec((tm, tn), out_map)],
        out_shape=[jax.ShapeDtypeStruct((m, d_model), F32), jax.ShapeDtypeStruct((m, d_model), BF16)],
        scratch_shapes=[pltpu.VMEM((2, tm, d_model), F32), pltpu.VMEM((tm, 1), F32), pltpu.VMEM((tm, 1), F32)],
        compiler_params=_params("arbitrary", "arbitrary"),
    )(*branches, w16, x32, ln_g[None, :], ln_b[None, :])


def _rope_tables(pos0, t_len):
    half = HEAD_DIM // 2
    inv = ROPE_BASE ** (-jnp.arange(half, dtype=F32) / half)
    ang = (pos0 + jnp.arange(t_len)).astype(F32)[:, None] * inv[None, :]
    cos = jnp.cos(ang)
    sin = jnp.sin(ang)
    return jnp.concatenate([cos, cos], axis=1), jnp.concatenate([-sin, sin], axis=1)


def _head_major(c, batch, t_len, n_h):
    return jnp.swapaxes(c.reshape(batch, t_len, -1)[:, :, :n_h], 1, 2)


def _layer(x32, x16, hist, weights, batch, t_len, pos0, alpha):
    (w_main16, w_f16, conv_w, conv_b, conv_ln_g, conv_ln_b, ret_gn_g, ret_gn_b, fox_bf,
     trev, w_out16, ln_g, ln_b, log_gamma) = weights
    width = conv_w.shape[1]
    n_h = width // HEAD_DIM
    rows = batch * t_len
    prompt = hist is None
    tm = min(rows, 512)

    z32, z16 = _inproj(x16, w_main16, min(rows, 1024), 1024)
    f_logits = _forget_logits(x16, w_f16, tm)
    bias_f = jnp.pad(fox_bf, (0, LANES - n_h))[None, :]
    zero_row = jnp.zeros((batch, 1, LANES), F32)

    if prompt:
        conv_buf = jnp.zeros((batch, CONV_WIDTH - 1, width), F32)
        ret_s0 = jnp.zeros((batch, n_h, HEAD_DIM, HEAD_DIM), F32)
        ret_blk = min(t_len, 512)
    else:
        conv_buf, ret_s0, fox_k_c, fox_v_c, fox_lf_c, band_k_c, band_v_c, layer = hist
        ret_blk = t_len

    mix_a, conv_state = _conv_branch(z32, conv_buf, conv_w, conv_b, conv_ln_g, conv_ln_b, batch, t_len, width)

    cos2, sin2 = _rope_tables(pos0, t_len)
    mix_b, ret_state = _retention(z32, z16, cos2, sin2, ret_gn_g, ret_gn_b, ret_s0.astype(F32), log_gamma,
                                  batch, t_len, ret_blk, 3)

    if prompt:
        logf, c = _cumsum(f_logits, bias_f, zero_row, batch, min(t_len, 256), True)
        mix_c = _fox_prompt(z32, z16, c, _head_major(c, batch, t_len, n_h), batch, t_len, n_h, 7,
                            min(t_len, 1024))
    else:
        past = fox_k_c.shape[2]
        lf_c = jnp.pad(fox_lf_c.astype(F32), ((0, 0), (0, 0), (0, LANES - n_h))).reshape(batch * past, LANES)
        _, c_cache = _cumsum(lf_c, bias_f, zero_row, batch, min(past, 256), False)
        init = c_cache.reshape(batch, past, LANES)[:, past - 1:, :]
        logf, c_new = _cumsum(f_logits, bias_f, init, batch, t_len, True)
        cq = _head_major(c_new, batch, t_len, n_h)[..., None]
        ckc = _head_major(c_cache, batch, past, n_h)[:, :, None, :]
        ckn = _head_major(c_new, batch, t_len, n_h)[:, :, None, :]
        mix_c = _fox_sample(z32, z16, fox_k_c, fox_v_c, layer, cq, ckc, ckn, batch, t_len, n_h, 7)

    if prompt:
        mix_d = _band_prompt(z32, z16, trev, batch, t_len, n_h, 11)
    else:
        mix_d = _band_sample(z32, z16, band_k_c, band_v_c, layer, trev, batch, t_len, n_h, 11, pos0)

    y32, y16 = _outproj((mix_a, mix_b, mix_c, mix_d), w_out16, x32, ln_g, ln_b, alpha, min(rows, 512))

    def heads(c0):
        return z32[:, c0 * width:(c0 + 1) * width].reshape(batch, t_len, n_h, HEAD_DIM)

    fox_k, fox_v = heads(8), heads(9)
    band_k, band_v = heads(12), heads(13)
    if prompt:
        keep = min(BAND_PAST, t_len)
        band_k, band_v = band_k[:, t_len - keep:], band_v[:, t_len - keep:]
    logf = logf.reshape(batch, t_len, LANES)[:, :, :n_h]
    return y32, y16, (conv_state, ret_state, fox_k, fox_v, logf, band_k, band_v)


def kernel(x_prompt, x_sample, cache_conv, state_ret, cache_fox_k, cache_fox_v, cache_fox_logf,
           cache_band_k, cache_band_v, w_in, conv_w, conv_b, conv_ln_g, conv_ln_b, ret_gn_g, ret_gn_b,
           fox_bf, rel_bias, w_out, ln_g, ln_b):
    depth = w_in.shape[0]
    alpha = (2.0 * depth) ** 0.25
    b_p, t_p, d_model = x_prompt.shape
    b_s, t_s, _ = x_sample.shape
    past = cache_fox_k.shape[2]
    width = conv_w.shape[2]
    n_h = width // HEAD_DIM
    f_col = 11 * width
    log_gamma = jnp.asarray(np.log1p(-np.exp2(-5.0 - np.arange(n_h))), F32)

    xp32 = x_prompt.reshape(b_p * t_p, d_model)
    xs32 = x_sample.reshape(b_s * t_s, d_model)
    xp16, xs16 = xp32.astype(BF16), xs32.astype(BF16)
    st_p, st_s = [], []
    for l in range(depth):
        w_main16 = jnp.concatenate([w_in[l, :, :f_col], w_in[l, :, f_col + n_h:]], axis=1).astype(BF16)
        w_f16 = jnp.pad(w_in[l, :, f_col:f_col + n_h], ((0, 0), (0, LANES - n_h))).astype(BF16)
        weights = (w_main16, w_f16, conv_w[l], conv_b[l], conv_ln_g[l], conv_ln_b[l], ret_gn_g[l], ret_gn_b[l],
                   fox_bf[l], _toeplitz_row(rel_bias[l]), w_out[l].astype(BF16), ln_g[l], ln_b[l], log_gamma)
        xp32, xp16, sp = _layer(xp32, xp16, None, weights, b_p, t_p, 0, alpha)
        hist = (cache_conv[l], state_ret[l], cache_fox_k, cache_fox_v, cache_fox_logf[l],
                cache_band_k, cache_band_v, l)
        xs32, xs16, ss = _layer(xs32, xs16, hist, weights, b_s, t_s, past, alpha)
        st_p.append(sp)
        st_s.append(ss)

    def stack(states, i):
        return jnp.stack([s[i] for s in states], axis=0)

    return (xp32.reshape(b_p, t_p, d_model), xs32.reshape(b_s, t_s, d_model),
            *[stack(st_p, i) for i in range(7)], *[stack(st_s, i) for i in range(7)])
```
